```python
import math
import jax
import jax.numpy as jnp
from jax import lax
import numpy as np

D_MODEL = 4096
BATCH = 2
SEQ = 4096
DEPTH = 2
DEC_BATCH = 8
DEC_SEQ = 2048
PAST_LEN = 128

GRID_W = 64
N_EVEN = (DEPTH + 1) // 2
N_ODD = DEPTH // 2
HEAD_DIM = 128
NA_HEADS = 16
NA_WIDTH = NA_HEADS * HEAD_DIM
NA_MAX_ROWS = 8
NA_COLS = 16
NA_QBLOCK = 16
NA_KBLOCK = 32
HY_WIDTH = D_MODEL - NA_WIDTH
HY_SHORT = 3
HY_BANDS = 16
HY_EMB = 2 * HY_BANDS + 1
HY_HIDDEN = 64
HY_DECAY_MIN = -math.log(1e-2) / 1.5
HY_DECAY_MAX = -math.log(1e-2) / 0.3
S5_GROUP = 16
S5_GROUPS = D_MODEL // S5_GROUP
S5_STATE = 64
X_HEADS = 4
X_WIDTH = X_HEADS * HEAD_DIM
N_MEM = 256
FFN_HIDDEN = ((8 * D_MODEL + 3 * 256 - 1) // (3 * 256)) * 256
EPS = 1e-6

kernel_name = "hybrid_na_hyena_s5_encoder"


def rms_norm(x, g):
    xf = x.astype(jnp.float32)
    y = xf * lax.rsqrt(jnp.mean(xf * xf, axis=-1, keepdims=True) + EPS)
    return (y * g.astype(jnp.float32)).astype(x.dtype)


def neighbourhood_attention(q, k, v, rpb):
    b, l, h, dh = q.shape
    rows = l // GRID_W
    kr = min(NA_MAX_ROWS, rows)
    n_cb = GRID_W // NA_QBLOCK
    qcol = np.arange(GRID_W).reshape(n_cb, NA_QBLOCK)
    qstart = np.clip(qcol - NA_COLS // 2, 0, GRID_W - NA_COLS)
    kstart = np.clip(np.arange(n_cb) * NA_QBLOCK - NA_COLS // 2, 0, GRID_W - NA_KBLOCK)
    key_cols = kstart[:, None] + np.arange(NA_KBLOCK)
    kc = key_cols[:, None, :]
    col_valid = (kc >= qstart[..., None]) & (kc < qstart[..., None] + NA_COLS)
    col_idx = np.clip(kc - qcol[..., None] + NA_COLS - 1, 0, 2 * NA_COLS - 2)
    scale = dh ** -0.5
    q_rows = q.reshape(b, rows, n_cb, NA_QBLOCK, h, dh).transpose(1, 0, 2, 3, 4, 5)
    k_grid = k.reshape(b, rows, GRID_W, h, dh)
    v_grid = v.reshape(b, rows, GRID_W, h, dh)

    def one_row(args):
        r, q_r = args
        r0 = jnp.clip(r - kr // 2, 0, rows - kr)
        k_r = lax.dynamic_slice_in_dim(k_grid, r0, kr, axis=1)[:, :, key_cols]
        v_r = lax.dynamic_slice_in_dim(v_grid, r0, kr, axis=1)[:, :, key_cols]
        s = jnp.einsum("bjqhd,bkjchd->bhjqkc", q_r, k_r, preferred_element_type=jnp.float32) * scale
        row_idx = r0 + jnp.arange(kr) - r + NA_MAX_ROWS - 1
        bias = rpb.astype(jnp.float32)[:, row_idx][:, :, col_idx].transpose(0, 2, 3, 1, 4)
        s = jnp.where(col_valid[:, :, None, :], s + bias, -jnp.inf)
        p = jax.nn.softmax(s, axis=(-2, -1)).astype(v.dtype)
        o = jnp.einsum("bhjqkc,bkjchd->bjqhd", p, v_r)
        return o.reshape(b, GRID_W, h * dh)

    out = lax.map(one_row, (jnp.arange(rows), q_rows))
    return out.transpose(1, 0, 2, 3).reshape(b, l, h * dh)


def centred_depthwise_conv(u, w, bias):
    y = lax.conv_general_dilated(u, w[:, None, :].astype(u.dtype), window_strides=(1,),
                                 padding=[(HY_SHORT // 2, HY_SHORT // 2)],
                                 dimension_numbers=("NWC", "WIO", "NWC"),
                                 feature_group_count=u.shape[-1])
    return y + bias.astype(u.dtype)


def implicit_filters(l, w1, b1, w2, b2, w3, b3, freq, log_decay):
    f32 = jnp.float32
    t = jnp.arange(l, dtype=f32) / l
    ang = 2.0 * math.pi * t[:, None] * jnp.arange(1, HY_BANDS + 1, dtype=f32)
    z = jnp.concatenate([t[:, None], jnp.cos(ang), jnp.sin(ang)], axis=-1)
    freq = freq.astype(f32)
    hid = jnp.sin(freq * (z @ w1.astype(f32) + b1.astype(f32)))
    hid = jnp.sin(freq * (hid @ w2.astype(f32) + b2.astype(f32)))
    filt = (hid @ w3.astype(f32) + b3.astype(f32)).reshape(l, 2, HY_WIDTH)
    filt = filt * jnp.exp(-jnp.exp(log_decay.astype(f32)) * t[:, None, None])
    return filt / jnp.sum(jnp.abs(filt), axis=(0, 1), keepdims=True)


def bidirectional_fft_conv(u, filt, bias):
    l = u.shape[1]
    h_fwd, h_bwd = filt[:, 0], filt[:, 1]
    taps = jnp.concatenate([h_fwd, jnp.zeros_like(h_fwd[:1]), h_bwd[:0:-1]], axis=0)
    u_f = jnp.fft.rfft(u.astype(jnp.float32), n=2 * l, axis=1)
    t_f = jnp.fft.rfft(taps, n=2 * l, axis=0)
    y = jnp.fft.irfft(u_f * t_f[None], n=2 * l, axis=1)[:, :l]
    return (y + u.astype(jnp.float32) * bias.astype(jnp.float32)).astype(u.dtype)


def hyena_mixer(z, conv_w, conv_b, f_w1, f_b1, f_w2, f_b2, f_w3, f_b3, f_freq, log_decay, hy_bias):
    l = z.shape[1]
    z = centred_depthwise_conv(z, conv_w, conv_b)
    x0, x1, v = jnp.split(z, 3, axis=-1)
    filt = implicit_filters(l, f_w1, f_b1, f_w2, f_b2, f_w3, f_b3, f_freq, log_decay)
    y = bidirectional_fft_conv(v * x1, filt, hy_bias)
    return y * x0


def even_mixer(xn, w_in, q_gain, k_gain, rpb, conv_w, conv_b, f_w1, f_b1, f_w2, f_b2, f_w3, f_b3,
               f_freq, log_decay, hy_bias, w_out):
    b, l, _ = xn.shape
    proj = xn @ w_in
    q, k, v, z = jnp.split(proj, [NA_WIDTH, 2 * NA_WIDTH, 3 * NA_WIDTH], axis=-1)
    q = rms_norm(q.reshape(b, l, NA_HEADS, HEAD_DIM), q_gain)
    k = rms_norm(k.reshape(b, l, NA_HEADS, HEAD_DIM), k_gain)
    v = v.reshape(b, l, NA_HEADS, HEAD_DIM)
    y_a = neighbourhood_attention(q, k, v, rpb)
    y_b = hyena_mixer(z, conv_w, conv_b, f_w1, f_b1, f_w2, f_b2, f_w3, f_b3, f_freq, log_decay, hy_bias)
    return jnp.concatenate([y_a, y_b], axis=-1) @ w_out


def _ssm_combine(left, right):
    a_l, b_l = left
    a_r, b_r = right
    return a_r * a_l, a_r * b_l + b_r


def s5_mixer(xn, lam_re, lam_im, log_step, b_re, b_im, c_re, c_im, d_skip, w_glu):
    f32 = jnp.float32
    bsz, l, d = xn.shape
    u = xn.astype(f32).reshape(bsz, l, S5_GROUPS, S5_GROUP)
    lam = lax.complex(jnp.minimum(lam_re.astype(f32), -1e-4), lam_im.astype(f32))
    step = jnp.exp(log_step.astype(f32))[..., None]
    lam_bar = jnp.exp(lam * step)
    b_bar = ((lam_bar - 1.0) / lam)[..., None] * lax.complex(b_re.astype(f32), b_im.astype(f32))
    c_mat = lax.complex(c_re.astype(f32), c_im.astype(f32))
    skip = d_skip.astype(f32)

    def one_sequence(u_s):
        y = skip * u_s
        for direction in range(2):
            bu = jnp.einsum("gpc,lgc->lgp", b_bar[direction], u_s)
            if direction == 1:
                bu = jnp.flip(bu, axis=0)
            a = jnp.broadcast_to(lam_bar[direction], bu.shape)
            _, states = lax.associative_scan(_ssm_combine, (a, bu), axis=0)
            if direction == 1:
                states = jnp.flip(states, axis=0)
            y = y + jnp.einsum("gcp,lgp->lgc", c_mat[direction], states).real
        return y

    y = lax.map(one_sequence, u).reshape(bsz, l, d)
    y = jax.nn.gelu(y).astype(xn.dtype)
    val, gate = jnp.split(y @ w_glu, 2, axis=-1)
    return val * jax.nn.sigmoid(gate)


def memory_cross_attention(xn, mem, g_mem, wq, wk, wv, wo, q_gain, k_gain):
    b, l, _ = xn.shape
    m = mem.shape[1]
    mn = rms_norm(mem, g_mem)
    q = rms_norm((xn @ wq).reshape(b, l, X_HEADS, HEAD_DIM), q_gain)
    k = rms_norm((mn @ wk).reshape(b, m, X_HEADS, HEAD_DIM), k_gain)
    v = (mn @ wv).reshape(b, m, X_HEADS, HEAD_DIM)
    s = jnp.einsum("blhd,bmhd->bhlm", q, k, preferred_element_type=jnp.float32) * HEAD_DIM ** -0.5
    p = jax.nn.softmax(s, axis=-1).astype(v.dtype)
    o = jnp.einsum("bhlm,bmhd->blhd", p, v).reshape(b, l, X_WIDTH)
    return o @ wo


def swiglu(xn, w_gate, w_up, w_down):
    return (jax.nn.silu(xn @ w_gate) * (xn @ w_up)) @ w_down


def setup_inputs(seed: int = 0) -> dict:
    key = jax.random.key(seed)
    keys = iter(jax.random.split(key, 64))
    f32 = jnp.float32

    def nrm(shape, scale=1.0):
        return jax.random.normal(next(keys), shape, f32) * scale

    def gain(shape):
        return 1.0 + nrm(shape, 0.05)

    d = D_MODEL
    decay0 = jnp.log(jnp.linspace(HY_DECAY_MIN, HY_DECAY_MAX, HY_WIDTH, dtype=f32))
    state_n = jnp.arange(S5_STATE, dtype=f32)
    mix_w = NA_WIDTH + HY_WIDTH
    return {
        "x_prompt": nrm((BATCH, SEQ, d)),
        "x_sample": nrm((DEC_BATCH, DEC_SEQ, d)),
        "mem_prompt": nrm((BATCH, N_MEM, d)),
        "mem_sample": nrm((DEC_BATCH, N_MEM, d)),
        "g_mix": gain((DEPTH, d)),
        "g_cross": gain((DEPTH, d)),
        "g_mem": gain((DEPTH, d)),
        "g_ffn": gain((DEPTH, d)),
        "w_in": nrm((N_EVEN, d, 3 * NA_WIDTH + 3 * HY_WIDTH), d ** -0.5),
        "na_q_gain": gain((N_EVEN, HEAD_DIM)),
        "na_k_gain": gain((N_EVEN, HEAD_DIM)),
        "na_rpb": nrm((N_EVEN, NA_HEADS, 2 * NA_MAX_ROWS - 1, 2 * NA_COLS - 1), 0.1),
        "hy_conv_w": nrm((N_EVEN, HY_SHORT, 3 * HY_WIDTH), HY_SHORT ** -0.5),
        "hy_conv_b": nrm((N_EVEN, 3 * HY_WIDTH), 0.01),
        "hy_f_w1": nrm((N_EVEN, HY_EMB, HY_HIDDEN), HY_EMB ** -0.5),
        "hy_f_b1": nrm((N_EVEN, HY_HIDDEN), 0.1),
        "hy_f_w2": nrm((N_EVEN, HY_HIDDEN, HY_HIDDEN), HY_HIDDEN ** -0.5),
        "hy_f_b2": nrm((N_EVEN, HY_HIDDEN), 0.1),
        "hy_f_w3": nrm((N_EVEN, HY_HIDDEN, 2 * HY_WIDTH), HY_HIDDEN ** -0.5),
        "hy_f_b3": nrm((N_EVEN, 2 * HY_WIDTH), 0.1),
        "hy_f_freq": gain((N_EVEN, HY_HIDDEN)),
        "hy_log_decay": decay0 + nrm((N_EVEN, 2, HY_WIDTH), 0.05),
        "hy_bias": nrm((N_EVEN, HY_WIDTH)),
        "w_out": nrm((N_EVEN, mix_w, d), mix_w ** -0.5),
        "s5_lam_re": -0.5 + nrm((N_ODD, 2, S5_GROUPS, S5_STATE), 0.01),
        "s5_lam_im": math.pi * state_n + nrm((N_ODD, 2, S5_GROUPS, S5_STATE), 0.01),
        "s5_log_step": jax.random.uniform(next(keys), (N_ODD, 2, S5_GROUPS), f32, math.log(1e-3), math.log(1e-1)),
        "s5_b_re": nrm((N_ODD, 2, S5_GROUPS, S5_STATE, S5_GROUP), (2 * S5_GROUP) ** -0.5),
        "s5_b_im": nrm((N_ODD, 2, S5_GROUPS, S5_STATE, S5_GROUP), (2 * S5_GROUP) ** -0.5),
        "s5_c_re": nrm((N_ODD, 2, S5_GROUPS, S5_GROUP, S5_STATE), (2 * S5_STATE) ** -0.5),
        "s5_c_im": nrm((N_ODD, 2, S5_GROUPS, S5_GROUP, S5_STATE), (2 * S5_STATE) ** -0.5),
        "s5_d": nrm((N_ODD, S5_GROUPS, S5_GROUP)),
        "w_glu": nrm((N_ODD, d, 2 * d), d ** -0.5),
        "x_wq": nrm((DEPTH, d, X_WIDTH), d ** -0.5),
        "x_wk": nrm((DEPTH, d, X_WIDTH), d ** -0.5),
        "x_wv": nrm((DEPTH, d, X_WIDTH), d ** -0.5),
        "x_wo": nrm((DEPTH, X_WIDTH, d), X_WIDTH ** -0.5),
        "x_q_gain": gain((DEPTH, HEAD_DIM)),
        "x_k_gain": gain((DEPTH, HEAD_DIM)),
        "w_ffn_gate": nrm((DEPTH, d, FFN_HIDDEN), d ** -0.5),
        "w_ffn_up": nrm((DEPTH, d, FFN_HIDDEN), d ** -0.5),
        "w_ffn_down": nrm((DEPTH, FFN_HIDDEN, d), FFN_HIDDEN ** -0.5),
    }


def reference(x_prompt, x_sample, mem_prompt, mem_sample, g_mix, g_cross, g_mem, g_ffn,
              w_in, na_q_gain, na_k_gain, na_rpb, hy_conv_w, hy_conv_b, hy_f_w1, hy_f_b1,
              hy_f_w2, hy_f_b2, hy_f_w3, hy_f_b3, hy_f_freq, hy_log_decay, hy_bias, w_out,
              s5_lam_re, s5_lam_im, s5_log_step, s5_b_re, s5_b_im, s5_c_re, s5_c_im, s5_d, w_glu,
              x_wq, x_wk, x_wv, x_wo, x_q_gain, x_k_gain, w_ffn_gate, w_ffn_up, w_ffn_down):

    def run(x, mem):
        for layer in range(DEPTH):
            i = layer // 2
            xn = rms_norm(x, g_mix[layer])
            if layer % 2 == 0:
                mix = even_mixer(xn, w_in[i], na_q_gain[i], na_k_gain[i], na_rpb[i], hy_conv_w[i],
                                 hy_conv_b[i], hy_f_w1[i], hy_f_b1[i], hy_f_w2[i], hy_f_b2[i],
                                 hy_f_w3[i], hy_f_b3[i], hy_f_freq[i], hy_log_decay[i], hy_bias[i],
                                 w_out[i])
            else:
                mix = s5_mixer(xn, s5_lam_re[i], s5_lam_im[i], s5_log_step[i], s5_b_re[i], s5_b_im[i],
                               s5_c_re[i], s5_c_im[i], s5_d[i], w_glu[i])
            x = x + mix
            x = x + memory_cross_attention(rms_norm(x, g_cross[layer]), mem, g_mem[layer], x_wq[layer],
                                           x_wk[layer], x_wv[layer], x_wo[layer], x_q_gain[layer],
                                           x_k_gain[layer])
            x = x + swiglu(rms_norm(x, g_ffn[layer]), w_ffn_gate[layer], w_ffn_up[layer], w_ffn_down[layer])
        return x

    y_prompt = run(x_prompt, mem_prompt)
    y_sample = run(x_sample, mem_sample)
    return (y_prompt, y_sample)
```

```python
import functools
import math

import jax
import jax.numpy as jnp
from jax import lax
from jax.experimental import pallas as pl
from jax.experimental.pallas import tpu as pltpu

F32 = jnp.float32
BF16 = jnp.bfloat16
EPS = 1e-6
HEAD_DIM = 128
GRID_W = 64
NA_ROWS = 8
NA_COLS = 16
NA_RBLK = 8
NA_WIN = 16
HY_BANDS = 16
S5_GROUP = 16
S5_TILE = 16
S5_SUB = 8
NEG = -1e30
HIGHEST = lax.Precision.HIGHEST
VMEM_LIMIT = 56 * 1024 * 1024


def _cp(sem, vmem=VMEM_LIMIT):
    return pltpu.CompilerParams(dimension_semantics=sem, vmem_limit_bytes=vmem)


def _pick(n, cands):
    for c in cands:
        if c <= n and n % c == 0:
            return c
    return n


def _sigmoid(x):
    return 1.0 / (1.0 + jnp.exp(-x))


def _rmsnorm_kernel(x_ref, g_ref, o_ref):
    x = x_ref[...].astype(F32)
    ms = jnp.mean(x * x, axis=-1, keepdims=True)
    o_ref[...] = (x * lax.rsqrt(ms + EPS) * g_ref[...]).astype(o_ref.dtype)


def rmsnorm(x, g, out_dtype):
    m, d = x.shape
    tm = _pick(m, (256, 128, 64, 32, 16, 8))
    return pl.pallas_call(
        _rmsnorm_kernel,
        grid=(m // tm,),
        in_specs=[pl.BlockSpec((tm, d), lambda i: (i, 0)), pl.BlockSpec((1, d), lambda i: (0, 0))],
        out_specs=pl.BlockSpec((tm, d), lambda i: (i, 0)),
        out_shape=jax.ShapeDtypeStruct((m, d), out_dtype),
        compiler_params=_cp(("parallel",)),
        name="rmsnorm",
    )(x, g.reshape(1, d).astype(F32))


def _mm_kernel(*refs, nb, epi, nk):
    a_ref = refs[0]
    b_refs = refs[1:1 + nb]
    pos = 1 + nb
    res_ref = None
    if epi in ("res", "glu_res"):
        res_ref = refs[pos]
        pos += 1
    o_ref = refs[pos]
    acc_refs = refs[pos + 1:]

    def finish(parts):
        if epi == "plain":
            out = parts[0]
        elif epi == "res":
            out = res_ref[...] + parts[0]
        elif epi == "swiglu":
            out = parts[0] * _sigmoid(parts[0]) * parts[1]
        else:
            out = res_ref[...] + parts[0] * _sigmoid(parts[1])
        o_ref[...] = out.astype(o_ref.dtype)

    a = a_ref[...]
    parts = [jnp.dot(a, b[...], preferred_element_type=F32) for b in b_refs]
    if nk == 1:
        finish(parts)
    else:
        k = pl.program_id(2)

        @pl.when(k == 0)
        def _():
            for acc, p in zip(acc_refs, parts):
                acc[...] = p

        @pl.when(k > 0)
        def _():
            for acc, p in zip(acc_refs, parts):
                acc[...] += p

        @pl.when(k == nk - 1)
        def _():
            finish([acc[...] for acc in acc_refs])


def matmul(a, bs, *, epi="plain", res=None, out_dtype=F32, tm=None, tn=None):
    m, k = a.shape
    n = bs[0].shape[1]
    nb = len(bs)
    tk = k if k <= 4096 else _pick(k, tuple(range(6144, 127, -128)))
    nk = k // tk
    if tm is None:
        tm = _pick(m, ((1024,) if tk <= 4096 else ()) + (512, 256, 128, 64, 32, 16, 8))
    if tn is None:
        tn = _pick(n, (512, 256, 128))
    in_specs = [pl.BlockSpec((tm, tk), lambda i, j, kk: (i, kk))]
    in_specs += [pl.BlockSpec((tk, tn), lambda i, j, kk: (kk, j)) for _ in bs]
    args = [a] + list(bs)
    if res is not None:
        in_specs.append(pl.BlockSpec((tm, tn), lambda i, j, kk: (i, j)))
        args.append(res)
    scratch = [pltpu.VMEM((tm, tn), F32) for _ in bs] if nk > 1 else []
    return pl.pallas_call(
        functools.partial(_mm_kernel, nb=nb, epi=epi, nk=nk),
        grid=(m // tm, n // tn, nk),
        in_specs=in_specs,
        out_specs=pl.BlockSpec((tm, tn), lambda i, j, kk: (i, j)),
        out_shape=jax.ShapeDtypeStruct((m, n), out_dtype),
        scratch_shapes=scratch,
        compiler_params=_cp(("parallel", "parallel", "arbitrary")),
        name="mm_" + epi,
    )(*args)


def _na_table_kernel(rpb_ref, o_ref, *, n_ri, n_ci):
    h = pl.program_id(0)
    w = GRID_W
    qc = lax.broadcasted_iota(jnp.int32, (w, w), 0)
    kc = lax.broadcasted_iota(jnp.int32, (w, w), 1)
    ci = kc - qc + (NA_COLS - 1)
    qstart = jnp.clip(qc - NA_COLS // 2, 0, w - NA_COLS)
    valid = (kc >= qstart) & (kc < qstart + NA_COLS)
    neg = jnp.full((w, w), NEG, F32)
    tiles = []
    for ri in range(n_ri):
        t = jnp.zeros((w, w), F32)
        for j in range(n_ci):
            t = jnp.where(ci == j, rpb_ref[h * (n_ri * n_ci) + ri * n_ci + j], t)
        tiles.append(jnp.where(valid, t, neg))
    for e in range(NA_WIN):
        j0 = min(max(e - NA_ROWS // 2, 0), NA_WIN - NA_ROWS)
        for jp in range(NA_WIN // 2):
            pair = []
            for j in (2 * jp, 2 * jp + 1):
                pair.append(tiles[j - e + NA_ROWS - 1] if j0 <= j < j0 + NA_ROWS else neg)
            o_ref[0, e, :, jp * 2 * w:(jp + 1) * 2 * w] = jnp.concatenate(pair, axis=1)


def na_table(rpb):
    h, n_ri, n_ci = rpb.shape
    return pl.pallas_call(
        functools.partial(_na_table_kernel, n_ri=n_ri, n_ci=n_ci),
        grid=(h,),
        in_specs=[pl.BlockSpec(memory_space=pltpu.SMEM)],
        out_specs=pl.BlockSpec((1, NA_WIN, GRID_W, NA_WIN * GRID_W), lambda i: (i, 0, 0, 0)),
        out_shape=jax.ShapeDtypeStruct((h, NA_WIN, GRID_W, NA_WIN * GRID_W), F32),
        compiler_params=_cp(("parallel",)),
        name="na_table",
    )(rpb.reshape(-1).astype(F32))


def _na_kernel(q_ref, k_ref, v_ref, qg_ref, kg_ref, tab_ref, o_ref, qn_ref, kn_ref, *, rows):
    l = rows * GRID_W
    ch = _pick(l, (512,))

    def norm(src, g_ref, dst):
        for c in range(l // ch):
            x = src[0, c * ch:(c + 1) * ch, :].astype(F32)
            ms = jnp.mean(x * x, axis=-1, keepdims=True)
            dst[c * ch:(c + 1) * ch, :] = (x * lax.rsqrt(ms + EPS) * g_ref[...]).astype(BF16)

    norm(q_ref, qg_ref, qn_ref)
    norm(k_ref, kg_ref, kn_ref)
    scale = HEAD_DIM ** -0.5
    nq = NA_RBLK * GRID_W
    nkeys = NA_WIN * GRID_W

    def body(i, carry):
        rb = i * NA_RBLK
        kw = jnp.clip(rb - NA_ROWS // 2, 0, rows - NA_WIN)
        e0 = rb - kw
        q0 = pl.multiple_of(rb * GRID_W, nq)
        k0 = pl.multiple_of(kw * GRID_W, 256)
        q = qn_ref[pl.ds(q0, nq), :]
        kk = kn_ref[pl.ds(k0, nkeys), :]
        vv = v_ref[0, pl.ds(k0, nkeys), :]
        s = lax.dot_general(q, kk, (((1,), (1,)), ((), ())), preferred_element_type=F32) * scale
        s = s.reshape(NA_RBLK, GRID_W, nkeys) + tab_ref[0, pl.ds(e0, NA_RBLK)]
        m = jnp.max(s, axis=-1, keepdims=True)
        p = jnp.exp(s - m)
        den = jnp.sum(p, axis=-1, keepdims=True)
        o = jnp.dot(p.reshape(nq, nkeys).astype(BF16), vv, preferred_element_type=F32)
        o = o * (1.0 / den).reshape(nq, 1)
        o_ref[0, pl.ds(q0, nq), :] = o.astype(o_ref.dtype)
        return carry

    lax.fori_loop(0, rows // NA_RBLK, body, 0)


def na_attention(proj3, q_gain, k_gain, table, heads):
    b, l, _ = proj3.shape
    rows = l // GRID_W
    assert rows % NA_RBLK == 0 and rows >= NA_WIN
    blk = (1, l, HEAD_DIM)
    return pl.pallas_call(
        functools.partial(_na_kernel, rows=rows),
        grid=(b, heads),
        in_specs=[
            pl.BlockSpec(blk, lambda i, h: (i, 0, h)),
            pl.BlockSpec(blk, lambda i, h: (i, 0, heads + h)),
            pl.BlockSpec(blk, lambda i, h: (i, 0, 2 * heads + h)),
            pl.BlockSpec((1, HEAD_DIM), lambda i, h: (0, 0)),
            pl.BlockSpec((1, HEAD_DIM), lambda i, h: (0, 0)),
            pl.BlockSpec((1, NA_WIN, GRID_W, NA_WIN * GRID_W), lambda i, h: (h, 0, 0, 0)),
        ],
        out_specs=pl.BlockSpec(blk, lambda i, h: (i, 0, h)),
        out_shape=jax.ShapeDtypeStruct((b, l, heads * HEAD_DIM), BF16),
        scratch_shapes=[pltpu.VMEM((l, HEAD_DIM), BF16), pltpu.VMEM((l, HEAD_DIM), BF16)],
        compiler_params=_cp(("parallel", "parallel")),
        name="na_attention",
    )(proj3, proj3, proj3, q_gain.reshape(1, HEAD_DIM).astype(F32),
      k_gain.reshape(1, HEAD_DIM).astype(F32), table)


def _filt_kernel(w1t_ref, w1c_ref, w1s_ref, b1_ref, w2_ref, b2_ref, w3_ref, b3_ref, freq_ref, ld_ref,
                 f_ref, sum_ref, *, l, tl):
    i = pl.program_id(0)
    t = (i * tl + lax.broadcasted_iota(jnp.int32, (tl, 1), 0)).astype(F32) / l
    bands = (lax.broadcasted_iota(jnp.int32, (1, HY_BANDS), 1) + 1).astype(F32)
    ang = 2.0 * math.pi * t * bands
    dot = functools.partial(jnp.dot, precision=HIGHEST, preferred_element_type=F32)
    pre = t * w1t_ref[...] + dot(jnp.cos(ang), w1c_ref[...]) + dot(jnp.sin(ang), w1s_ref[...]) + b1_ref[...]
    freq = freq_ref[...]
    hid = jnp.sin(freq * pre)
    hid = jnp.sin(freq * (dot(hid, w2_ref[...]) + b2_ref[...]))
    f = dot(hid, w3_ref[...]) + b3_ref[...]
    f = f * jnp.exp(-jnp.exp(ld_ref[...]) * t)
    f_ref[...] = f
    part = jnp.sum(jnp.abs(f), axis=0, keepdims=True)

    @pl.when(i == 0)
    def _():
        sum_ref[...] = part

    @pl.when(i > 0)
    def _():
        sum_ref[...] += part


def _filt_combine_kernel(f_ref, sum_ref, f1_ref, f2_ref, ny_ref, *, c, tl):
    i = pl.program_id(0)
    tot = sum_ref[:, :c] + sum_ref[:, c:]
    inv = 1.0 / tot
    n = i * tl + lax.broadcasted_iota(jnp.int32, (tl, 1), 0)
    hf = f_ref[:, :c] * inv
    hb = jnp.where(n == 0, 0.0, f_ref[:, c:] * inv)
    f1 = hf + hb
    f1_ref[...] = f1.astype(f1_ref.dtype)
    f2_ref[...] = (hb - hf).astype(f2_ref.dtype)
    sign = (1 - 2 * (n & 1)).astype(F32)
    part = jnp.sum(f1 * sign, axis=0, keepdims=True)

    @pl.when(i == 0)
    def _():
        ny_ref[...] = part

    @pl.when(i > 0)
    def _():
        ny_ref[...] += part


def hyena_filters(l, w1, b1, w2, b2, w3, b3, freq, log_decay):
    hid = w2.shape[0]
    c2 = w3.shape[1]
    c = c2 // 2
    tl = _pick(l, (256,))
    row = lambda a: a.reshape(1, -1).astype(F32)
    full = lambda shape: pl.BlockSpec(shape, lambda i: (0, 0))
    f, s = pl.pallas_call(
        functools.partial(_filt_kernel, l=l, tl=tl),
        grid=(l // tl,),
        in_specs=[full((1, hid)), full((HY_BANDS, hid)), full((HY_BANDS, hid)), full((1, hid)),
                  full((hid, hid)), full((1, hid)), full((hid, c2)), full((1, c2)), full((1, hid)), full((1, c2))],
        out_specs=[pl.BlockSpec((tl, c2), lambda i: (i, 0)), full((1, c2))],
        out_shape=[jax.ShapeDtypeStruct((l, c2), F32), jax.ShapeDtypeStruct((1, c2), F32)],
        compiler_params=_cp(("arbitrary",)),
        name="hyena_filter_mlp",
    )(w1[0:1].astype(F32), w1[1:1 + HY_BANDS].astype(F32), w1[1 + HY_BANDS:].astype(F32), row(b1),
      w2.astype(F32), row(b2), w3.astype(F32), row(b3), row(freq), row(log_decay))
    return pl.pallas_call(
        functools.partial(_filt_combine_kernel, c=c, tl=tl),
        grid=(l // tl,),
        in_specs=[pl.BlockSpec((tl, c2), lambda i: (i, 0)), full((1, c2))],
        out_specs=[pl.BlockSpec((tl, c), lambda i: (i, 0)), pl.BlockSpec((tl, c), lambda i: (i, 0)), full((1, c))],
        out_shape=[jax.ShapeDtypeStruct((l, c), BF16), jax.ShapeDtypeStruct((l, c), BF16),
                   jax.ShapeDtypeStruct((1, c), F32)],
        compiler_params=_cp(("arbitrary",)),
        name="hyena_filter_combine",
    )(f, s)


def _conv_gate_kernel(z0_ref, z1_ref, z2_ref, w0_ref, w1_ref, w2_ref, b0_ref, b1_ref, b2_ref,
                      u_ref, x0_ref, *, l, ch):
    tc = z0_ref.shape[-1]

    def conv(z_ref, w_ref, b_ref, s):
        z = z_ref[0, s:s + ch, :].astype(F32)
        rid = lax.broadcasted_iota(jnp.int32, (ch, tc), 0)
        prev = z_ref[0, s - 1:s, :].astype(F32) if s > 0 else jnp.zeros((1, tc), F32)
        nxt = z_ref[0, s + ch:s + ch + 1, :].astype(F32) if s + ch < l else jnp.zeros((1, tc), F32)
        up = jnp.where(rid == 0, prev, pltpu.roll(z, 1, 0))
        dn = jnp.where(rid == ch - 1, nxt, pltpu.roll(z, ch - 1, 0))
        return w_ref[0:1, :] * up + w_ref[1:2, :] * z + w_ref[2:3, :] * dn + b_ref[...]

    for s in range(0, l, ch):
        x0 = conv(z0_ref, w0_ref, b0_ref, s)
        x1 = conv(z1_ref, w1_ref, b1_ref, s)
        v = conv(z2_ref, w2_ref, b2_ref, s)
        u_ref[0, s:s + ch, :] = (v * x1).astype(u_ref.dtype)
        x0_ref[0, s:s + ch, :] = x0.astype(x0_ref.dtype)


def hyena_conv_gate(proj3, conv_w, conv_b, base, c):
    b, l, _ = proj3.shape
    tc = _pick(math.gcd(base, c), (256, 128))
    ch = _pick(l, (512,))
    nb0, nbc = base // tc, c // tc
    zspec = lambda part: pl.BlockSpec((1, l, tc), lambda i, j: (i, 0, nb0 + part * nbc + j))
    wspec = lambda part: pl.BlockSpec((3, tc), lambda i, j: (0, part * nbc + j))
    bspec = lambda part: pl.BlockSpec((1, tc), lambda i, j: (0, part * nbc + j))
    ospec = pl.BlockSpec((1, l, tc), lambda i, j: (i, 0, j))
    cw = conv_w.astype(F32)
    cb = conv_b.reshape(1, -1).astype(F32)
    return pl.pallas_call(
        functools.partial(_conv_gate_kernel, l=l, ch=ch),
        grid=(b, nbc),
        in_specs=[zspec(0), zspec(1), zspec(2), wspec(0), wspec(1), wspec(2), bspec(0), bspec(1), bspec(2)],
        out_specs=[ospec, ospec],
        out_shape=[jax.ShapeDtypeStruct((b, l, c), BF16), jax.ShapeDtypeStruct((b, l, c), BF16)],
        compiler_params=_cp(("parallel", "parallel")),
        name="hyena_conv_gate",
    )(proj3, proj3, proj3, cw, cw, cw, cb, cb, cb)


def dft_tables(l):
    k = lax.broadcasted_iota(jnp.int32, (l, l), 0)
    t = lax.broadcasted_iota(jnp.int32, (l, l), 1)
    ang = ((k * t) % (2 * l)).astype(F32) * (math.pi / l)
    return jnp.cos(ang).astype(BF16), jnp.sin(ang).astype(BF16)


def _dft_fwd_kernel(c_ref, s_ref, u_ref, tre_ref, tim_ref, z1_ref, z2_ref, ny_ref, *, l, tk):
    kb = pl.program_id(2)
    u = u_ref[0]
    p = jnp.dot(c_ref[...], u, preferred_element_type=F32)
    q = jnp.dot(s_ref[...], u, preferred_element_type=F32)
    kidx = kb * tk + lax.broadcasted_iota(jnp.int32, (tk, 1), 0)
    sc = jnp.where(kidx == 0, 0.5 / l, 1.0 / l)
    tre = tre_ref[...] * sc
    tim = tim_ref[...] * sc
    z1_ref[0] = (p * tre + q * tim).astype(z1_ref.dtype)
    z2_ref[0] = (q * tre - p * tim).astype(z2_ref.dtype)

    @pl.when(kb == 0)
    def _():
        ch = _pick(l, (512,))
        acc = jnp.zeros((1, u_ref.shape[-1]), F32)
        for s in range(0, l, ch):
            n = s + lax.broadcasted_iota(jnp.int32, (ch, 1), 0)
            sign = (1 - 2 * (n & 1)).astype(F32)
            acc = acc + jnp.sum(u_ref[0, s:s + ch, :].astype(F32) * sign, axis=0, keepdims=True)
        ny_ref[0] = acc


def _dft_inv_kernel(c_ref, s_ref, z1_ref, z2_ref, ny_ref, tny_ref, u_ref, x0_ref, hb_ref, o_ref, *, l, tt):
    tb = pl.program_id(2)
    y = jnp.dot(c_ref[...], z1_ref[0], preferred_element_type=F32)
    y = y + jnp.dot(s_ref[...], z2_ref[0], preferred_element_type=F32)
    tidx = tb * tt + lax.broadcasted_iota(jnp.int32, (tt, 1), 0)
    sign = (1 - 2 * (tidx & 1)).astype(F32)
    y = y + sign * (ny_ref[0] * tny_ref[...] * (0.5 / l))
    y = y + u_ref[0].astype(F32) * hb_ref[...]
    o_ref[0] = (y * x0_ref[0].astype(F32)).astype(o_ref.dtype)


def hyena_long_conv(u, x0, cmat, smat, tre, tim, tny, hy_bias):
    b, l, c = u.shape
    tc = _pick(c, (512, 256, 128))
    tk = _pick(l, (512,))
    z1, z2, ny = pl.pallas_call(
        functools.partial(_dft_fwd_kernel, l=l, tk=tk),
        grid=(b, c // tc, l // tk),
        in_specs=[
            pl.BlockSpec((tk, l), lambda i, j, kb: (kb, 0)),
            pl.BlockSpec((tk, l), lambda i, j, kb: (kb, 0)),
            pl.BlockSpec((1, l, tc), lambda i, j, kb: (i, 0, j)),
            pl.BlockSpec((tk, tc), lambda i, j, kb: (kb, j)),
            pl.BlockSpec((tk, tc), lambda i, j, kb: (kb, j)),
        ],
        out_specs=[
            pl.BlockSpec((1, tk, tc), lambda i, j, kb: (i, kb, j)),
            pl.BlockSpec((1, tk, tc), lambda i, j, kb: (i, kb, j)),
            pl.BlockSpec((1, 1, tc), lambda i, j, kb: (i, 0, j)),
        ],
        out_shape=[jax.ShapeDtypeStruct((b, l, c), BF16), jax.ShapeDtypeStruct((b, l, c), BF16),
                   jax.ShapeDtypeStruct((b, 1, c), F32)],
        compiler_params=_cp(("parallel", "parallel", "arbitrary")),
        name="hyena_dft_fwd",
    )(cmat, smat, u, tre, tim)
    tt = tk
    return pl.pallas_call(
        functools.partial(_dft_inv_kernel, l=l, tt=tt),
        grid=(b, c // tc, l // tt),
        in_specs=[
            pl.BlockSpec((tt, l), lambda i, j, tb: (tb, 0)),
            pl.BlockSpec((tt, l), lambda i, j, tb: (tb, 0)),
            pl.BlockSpec((1, l, tc), lambda i, j, tb: (i, 0, j)),
            pl.BlockSpec((1, l, tc), lambda i, j, tb: (i, 0, j)),
            pl.BlockSpec((1, 1, tc), lambda i, j, tb: (i, 0, j)),
            pl.BlockSpec((1, tc), lambda i, j, tb: (0, j)),
            pl.BlockSpec((1, tt, tc), lambda i, j, tb: (i, tb, j)),
            pl.BlockSpec((1, tt, tc), lambda i, j, tb: (i, tb, j)),
            pl.BlockSpec((1, tc), lambda i, j, tb: (0, j)),
        ],
        out_specs=pl.BlockSpec((1, tt, tc), lambda i, j, tb: (i, tb, j)),
        out_shape=jax.ShapeDtypeStruct((b, l, c), BF16),
        compiler_params=_cp(("parallel", "parallel", "arbitrary")),
        name="hyena_dft_inv",
    )(cmat, smat, z1, z2, ny, tny, u, x0, hy_bias.reshape(1, c).astype(F32))


def _s5_prep_kernel(lre_ref, lim_ref, ls_ref, bre_ref, bim_ref, are_ref, aim_ref, obre_ref, obim_ref):
    lr = jnp.minimum(lre_ref[...], -1e-4)
    li = lim_ref[...]
    step = jnp.exp(ls_ref[...])
    mag = jnp.exp(lr * step)
    ar = mag * jnp.cos(li * step)
    ai = mag * jnp.sin(li * step)
    are_ref[...] = ar
    aim_ref[...] = ai
    den = lr * lr + li * li
    nr = ar - 1.0
    cr = (nr * lr + ai * li) / den
    cim = (ai * lr - nr * li) / den
    br = bre_ref[...]
    bi = bim_ref[...]
    obre_ref[...] = cr * br - cim * bi
    obim_ref[...] = cr * bi + cim * br


def s5_prep(lam_re, lam_im, log_step, b_re, b_im):
    _, g, p = lam_re.shape
    cg = b_re.shape[-1]
    n = 2 * g
    tg = _pick(n, (64, 32, 16, 8))
    flat = lambda a: a.reshape(n, 1, p).astype(F32)
    bt = lambda a: a.reshape(n, p, cg).transpose(0, 2, 1).astype(F32)
    s2 = pl.BlockSpec((tg, 1, p), lambda i: (i, 0, 0))
    s3 = pl.BlockSpec((tg, cg, p), lambda i: (i, 0, 0))
    return pl.pallas_call(
        _s5_prep_kernel,
        grid=(n // tg,),
        in_specs=[s2, s2, pl.BlockSpec((tg, 1, 1), lambda i: (i, 0, 0)), s3, s3],
        out_specs=[s2, s2, s3, s3],
        out_shape=[jax.ShapeDtypeStruct((n, 1, p), F32)] * 2 + [jax.ShapeDtypeStruct((n, cg, p), F32)] * 2,
        compiler_params=_cp(("parallel",)),
        name="s5_discretise",
    )(flat(lam_re), flat(lam_im), log_step.reshape(n, 1, 1).astype(F32), bt(b_re), bt(b_im))


def _gelu(x):
    return 0.5 * x * (1.0 + jnp.tanh(math.sqrt(2.0 / math.pi) * (x + 0.044715 * (x * x * x))))


def _s5_kernel(u_ref, wb_ref, wc_ref, ar_ref, ai_ref, extra_ref, o_ref, x_ref, carry_ref, *, reverse, last):
    ci = pl.program_id(1)
    tr = u_ref.shape[0]
    ns = x_ref.shape[1] // 2

    @pl.when(ci == 0)
    def _():
        carry_ref[...] = jnp.zeros_like(carry_ref)

    u = u_ref[...]
    x_ref[...] = jnp.dot(u.astype(BF16), wb_ref[0, 0], preferred_element_type=F32)
    ar = jnp.broadcast_to(ar_ref[0, 0], (S5_SUB, ns))
    ai = jnp.broadcast_to(ai_ref[0, 0], (S5_SUB, ns))
    nrow = tr // S5_SUB

    def body(i, c):
        r = (nrow - 1 - i) if reverse else i
        off = pl.multiple_of(r * S5_SUB, S5_SUB)
        xr, xi = c
        nr = ar * xr - ai * xi + x_ref[pl.ds(off, S5_SUB), :ns]
        ni = ar * xi + ai * xr + x_ref[pl.ds(off, S5_SUB), ns:]
        x_ref[pl.ds(off, S5_SUB), :ns] = nr
        x_ref[pl.ds(off, S5_SUB), ns:] = ni
        return nr, ni

    xr, xi = lax.fori_loop(0, nrow, body, (carry_ref[:, :ns], carry_ref[:, ns:]), unroll=2)
    carry_ref[:, :ns] = xr
    carry_ref[:, ns:] = xi
    y = jnp.dot(x_ref[...].astype(BF16), wc_ref[0, 0], preferred_element_type=F32)
    if last:
        o_ref[...] = _gelu(extra_ref[...] + y).astype(o_ref.dtype)
    else:
        o_ref[...] = (extra_ref[...] * u + y).astype(o_ref.dtype)


def s5_scan_pass(u_tm, wb, wc, ar, ai, extra, *, direction, last):
    rows, d = u_tm.shape
    nt = wb.shape[1]
    tw = d // nt
    ns2 = wb.shape[-1]
    tr = _pick(rows, (512, 256, 128, 64))
    nc = rows // tr
    reverse = direction == 1
    rmap = (lambda t, c: (nc - 1 - c, t)) if reverse else (lambda t, c: (c, t))
    wspec = lambda shape: pl.BlockSpec((1, 1) + shape, lambda t, c: (direction, t, 0, 0))
    espec = pl.BlockSpec((tr, tw), rmap) if last else pl.BlockSpec((1, tw), lambda t, c: (0, t))
    return pl.pallas_call(
        functools.partial(_s5_kernel, reverse=reverse, last=last),
        grid=(nt, nc),
        in_specs=[pl.BlockSpec((tr, tw), rmap), wspec((tw, ns2)), wspec((ns2, tw)),
                  wspec((1, ns2 // 2)), wspec((1, ns2 // 2)), espec],
        out_specs=pl.BlockSpec((tr, tw), rmap),
        out_shape=jax.ShapeDtypeStruct((rows, d), BF16 if last else F32),
        scratch_shapes=[pltpu.VMEM((tr, ns2), F32), pltpu.VMEM((S5_SUB, ns2), F32)],
        compiler_params=_cp(("parallel", "arbitrary")),
        name="s5_scan_bwd" if reverse else "s5_scan_fwd",
    )(u_tm, wb, wc, ar, ai, extra)


def s5_weights(lam_re, lam_im, log_step, b_re, b_im, c_re, c_im):
    _, g, p = lam_re.shape
    cg = b_re.shape[-1]
    nt = g // S5_TILE
    a_re, a_im, bb_re, bb_im = s5_prep(lam_re, lam_im, log_step, b_re, b_im)
    eye = jnp.eye(S5_TILE, dtype=F32)

    def in_map(bb):
        x = bb.reshape(2, nt, S5_TILE, cg, 1, p) * eye[None, None, :, None, :, None]
        return x.reshape(2, nt, S5_TILE * cg, S5_TILE * p)

    def out_map(cc):
        x = cc.astype(F32).reshape(2, nt, S5_TILE, cg, p).transpose(0, 1, 2, 4, 3)
        x = x[:, :, :, :, None, :] * eye[None, None, :, None, :, None]
        return x.reshape(2, nt, S5_TILE * p, S5_TILE * cg)

    wb = jnp.concatenate([in_map(bb_re), in_map(bb_im)], axis=-1).astype(BF16)
    wc = jnp.concatenate([out_map(c_re), -out_map(c_im)], axis=-2).astype(BF16)
    ar = a_re.reshape(2, nt, 1, S5_TILE * p)
    ai = a_im.reshape(2, nt, 1, S5_TILE * p)
    return wb, wc, ar, ai


def s5_mixer_core(xn, s5w, d_skip):
    b, l, d = xn.shape
    wb, wc, ar, ai = s5w
    assert b <= S5_SUB
    u_tm = jnp.transpose(xn, (1, 0, 2))
    u_tm = jnp.pad(u_tm, ((0, 0), (0, S5_SUB - b), (0, 0))).reshape(l * S5_SUB, d)
    skip = d_skip.reshape(1, d).astype(F32)
    y1 = s5_scan_pass(u_tm, wb, wc, ar, ai, skip, direction=0, last=False)
    y = s5_scan_pass(u_tm, wb, wc, ar, ai, y1, direction=1, last=True)
    y = y.reshape(l, S5_SUB, d)[:, :b]
    return jnp.transpose(y, (1, 0, 2)).reshape(b * l, d)


def _xattn_kernel(q_ref, k_ref, v_ref, qg_ref, kg_ref, o_ref, *, heads):
    scale = HEAD_DIM ** -0.5

    def hnorm(x, g_ref):
        ms = jnp.mean(x * x, axis=-1, keepdims=True)
        return (x * lax.rsqrt(ms + EPS) * g_ref[...]).astype(BF16)

    for h in range(heads):
        sl = slice(h * HEAD_DIM, (h + 1) * HEAD_DIM)
        q = hnorm(q_ref[0, :, sl], qg_ref)
        k = hnorm(k_ref[0, :, sl], kg_ref)
        s = lax.dot_general(q, k, (((1,), (1,)), ((), ())), preferred_element_type=F32) * scale
        m = jnp.max(s, axis=-1, keepdims=True)
        p = jnp.exp(s - m)
        den = jnp.sum(p, axis=-1, keepdims=True)
        o = jnp.dot(p.astype(BF16), v_ref[0, :, sl].astype(BF16), preferred_element_type=F32)
        o_ref[0, :, sl] = (o * (1.0 / den)).astype(o_ref.dtype)


def cross_attention_core(q, k, v, q_gain, k_gain):
    b, l, xw = q.shape
    mm = k.shape[1]
    tq = _pick(l, (512, 256, 128))
    g = lambda a: a.reshape(1, HEAD_DIM).astype(F32)
    return pl.pallas_call(
        functools.partial(_xattn_kernel, heads=xw // HEAD_DIM),
        grid=(b, l // tq),
        in_specs=[
            pl.BlockSpec((1, tq, xw), lambda i, j: (i, j, 0)),
            pl.BlockSpec((1, mm, xw), lambda i, j: (i, 0, 0)),
            pl.BlockSpec((1, mm, xw), lambda i, j: (i, 0, 0)),
            pl.BlockSpec((1, HEAD_DIM), lambda i, j: (0, 0)),
            pl.BlockSpec((1, HEAD_DIM), lambda i, j: (0, 0)),
        ],
        out_specs=pl.BlockSpec((1, tq, xw), lambda i, j: (i, j, 0)),
        out_shape=jax.ShapeDtypeStruct((b, l, xw), BF16),
        compiler_params=_cp(("parallel", "parallel")),
        name="cross_attention",
    )(q, k, v, g(q_gain), g(k_gain))


def kernel(x_prompt, x_sample, mem_prompt, mem_sample, g_mix, g_cross, g_mem, g_ffn, w_in, na_q_gain, na_k_gain, na_rpb, hy_conv_w, hy_conv_b, hy_f_w1, hy_f_b1, hy_f_w2, hy_f_b2, hy_f_w3, hy_f_b3, hy_f_freq, hy_log_decay, hy_bias, w_out, s5_lam_re, s5_lam_im, s5_log_step, s5_b_re, s5_b_im, s5_c_re, s5_c_im, s5_d, w_glu, x_wq, x_wk, x_wv, x_wo, x_q_gain, x_k_gain, w_ffn_gate, w_ffn_up, w_ffn_down):
    depth, d = g_mix.shape
    heads = na_rpb.shape[1]
    naw = heads * HEAD_DIM
    hyw = hy_bias.shape[-1]
    bf = lambda w: w.astype(BF16)

    layers = []
    for layer in range(depth):
        i = layer // 2
        p = {}
        if layer % 2 == 0:
            p["w_in"] = bf(w_in[i])
            p["w_out"] = bf(w_out[i])
            p["table"] = na_table(na_rpb[i])
            p["filters"] = {}
        else:
            p["s5w"] = s5_weights(s5_lam_re[i], s5_lam_im[i], s5_log_step[i], s5_b_re[i], s5_b_im[i],
                                  s5_c_re[i], s5_c_im[i])
            p["w_val"] = bf(w_glu[i][:, :d])
            p["w_gate"] = bf(w_glu[i][:, d:])
        p["wq"], p["wk"], p["wv"], p["wo"] = bf(x_wq[layer]), bf(x_wk[layer]), bf(x_wv[layer]), bf(x_wo[layer])
        p["ffn"] = (bf(w_ffn_gate[layer]), bf(w_ffn_up[layer]), bf(w_ffn_down[layer]))
        layers.append(p)
    tables = {}

    def filters_for(layer, l):
        p = layers[layer]
        if l not in p["filters"]:
            i = layer // 2
            if l not in tables:
                tables[l] = dft_tables(l)
            cmat, smat = tables[l]
            f1, f2, ny = hyena_filters(l, hy_f_w1[i], hy_f_b1[i], hy_f_w2[i], hy_f_b2[i], hy_f_w3[i],
                                       hy_f_b3[i], hy_f_freq[i], hy_log_decay[i].reshape(-1))
            tre = matmul(cmat, [f1], out_dtype=F32)
            tim = matmul(smat, [f2], out_dtype=F32)
            p["filters"][l] = (cmat, smat, tre, tim, ny)
        return p["filters"][l]

    def run(x, mem):
        b, l, _ = x.shape
        nm = mem.shape[1]
        x = x.reshape(b * l, d)
        mem2 = mem.reshape(b * nm, d)
        for layer in range(depth):
            i = layer // 2
            p = layers[layer]
            if layer % 2 == 0:
                xn = rmsnorm(x, g_mix[layer], BF16)
                proj = matmul(xn, [p["w_in"]], out_dtype=BF16)
                proj3 = proj.reshape(b, l, -1)
                y_a = na_attention(proj3, na_q_gain[i], na_k_gain[i], p["table"], heads)
                cmat, smat, tre, tim, tny = filters_for(layer, l)
                u, x0 = hyena_conv_gate(proj3, hy_conv_w[i], hy_conv_b[i], 3 * naw, hyw)
                y_b = hyena_long_conv(u, x0, cmat, smat, tre, tim, tny, hy_bias[i])
                ymix = jnp.concatenate([y_a, y_b], axis=-1).reshape(b * l, naw + hyw)
                x = matmul(ymix, [p["w_out"]], epi="res", res=x)
            else:
                xn = rmsnorm(x, g_mix[layer], F32)
                y = s5_mixer_core(xn.reshape(b, l, d), p["s5w"], s5_d[i])
                x = matmul(y, [p["w_val"], p["w_gate"]], epi="glu_res", res=x)
            xn = rmsnorm(x, g_cross[layer], BF16)
            mn = rmsnorm(mem2, g_mem[layer], BF16)
            q = matmul(xn, [p["wq"]], out_dtype=F32)
            k = matmul(mn, [p["wk"]], out_dtype=F32)
            v = matmul(mn, [p["wv"]], out_dtype=F32)
            xw = q.shape[-1]
            o = cross_attention_core(q.reshape(b, l, xw), k.reshape(b, nm, xw), v.reshape(b, nm, xw),
                                     x_q_gain[layer], x_k_gain[layer])
            x = matmul(o.reshape(b * l, xw), [p["wo"]], epi="res", res=x)
            xn = rmsnorm(x, g_ffn[layer], BF16)
            wg, wu, wd = p["ffn"]
            hdn = matmul(xn, [wg, wu], epi="swiglu", out_dtype=BF16)
            x = matmul(hdn, [wd], epi="res", res=x)
        return x.reshape(b, l, d)

    return run(x_prompt, mem_prompt), run(x_sample, mem_sample)
```

```python
import functools
import math

import jax
import jax.numpy as jnp
from jax import lax
from jax.experimental import pallas as pl
from jax.experimental.pallas import tpu as pltpu

F32 = jnp.float32
BF16 = jnp.bfloat16
EPS = 1e-6
HEAD_DIM = 128
GRID_W = 64
NA_ROWS = 8
NA_COLS = 16
NA_RBLK = 8
NA_WIN = 16
HY_BANDS = 16
S5_GROUP = 16
S5_T = 16
NEG = -1e30
HIGHEST = lax.Precision.HIGHEST
VMEM_LIMIT = 56 * 1024 * 1024


def _cp(sem, vmem=VMEM_LIMIT):
    return pltpu.CompilerParams(dimension_semantics=sem, vmem_limit_bytes=vmem)


def _pick(n, cands):
    for c in cands:
        if c <= n and n % c == 0:
            return c
    return n


def _sigmoid(x):
    return 1.0 / (1.0 + jnp.exp(-x))


def _rmsnorm_kernel(x_ref, g_ref, o_ref):
    x = x_ref[...].astype(F32)
    ms = jnp.mean(x * x, axis=-1, keepdims=True)
    o_ref[...] = (x * lax.rsqrt(ms + EPS) * g_ref[...]).astype(o_ref.dtype)


def rmsnorm(x, g, out_dtype):
    m, d = x.shape
    tm = _pick(m, (256, 128, 64, 32, 16, 8))
    return pl.pallas_call(
        _rmsnorm_kernel,
        grid=(m // tm,),
        in_specs=[pl.BlockSpec((tm, d), lambda i: (i, 0)), pl.BlockSpec((1, d), lambda i: (0, 0))],
        out_specs=pl.BlockSpec((tm, d), lambda i: (i, 0)),
        out_shape=jax.ShapeDtypeStruct((m, d), out_dtype),
        compiler_params=_cp(("parallel",)),
        name="rmsnorm",
    )(x, g.reshape(1, d).astype(F32))


def _mm_kernel(*refs, nb, epi):
    a_ref = refs[0]
    b_refs = refs[1:1 + nb]
    res_ref = refs[1 + nb] if epi in ("res", "glu_res") else None
    o_ref = refs[-1]
    a = a_ref[...]
    parts = [jnp.dot(a, b[...], preferred_element_type=F32) for b in b_refs]
    if epi == "plain":
        out = parts[0]
    elif epi == "res":
        out = res_ref[...] + parts[0]
    elif epi == "swiglu":
        out = parts[0] * _sigmoid(parts[0]) * parts[1]
    else:
        out = res_ref[...] + parts[0] * _sigmoid(parts[1])
    o_ref[...] = out.astype(o_ref.dtype)


def matmul(a, bs, *, epi="plain", res=None, out_dtype=F32):
    m, k = a.shape
    n = bs[0].shape[1]
    wide = k <= 4096
    tm = _pick(m, ((1024,) if wide else ()) + (512, 256, 128, 64, 32, 16, 8))
    tn = _pick(n, ((512,) if wide else ()) + (256, 128))
    in_specs = [pl.BlockSpec((tm, k), lambda i, j: (i, 0))]
    in_specs += [pl.BlockSpec((k, tn), lambda i, j: (0, j)) for _ in bs]
    args = [a] + list(bs)
    if res is not None:
        in_specs.append(pl.BlockSpec((tm, tn), lambda i, j: (i, j)))
        args.append(res)
    return pl.pallas_call(
        functools.partial(_mm_kernel, nb=len(bs), epi=epi),
        grid=(m // tm, n // tn),
        in_specs=in_specs,
        out_specs=pl.BlockSpec((tm, tn), lambda i, j: (i, j)),
        out_shape=jax.ShapeDtypeStruct((m, n), out_dtype),
        compiler_params=_cp(("parallel", "parallel")),
        name="mm_" + epi,
    )(*args)


def _na_table_kernel(rpb_ref, o_ref, *, n_ri, n_ci):
    h = pl.program_id(0)
    w = GRID_W
    qc = lax.broadcasted_iota(jnp.int32, (w, w), 0)
    kc = lax.broadcasted_iota(jnp.int32, (w, w), 1)
    ci = kc - qc + (NA_COLS - 1)
    qstart = jnp.clip(qc - NA_COLS // 2, 0, w - NA_COLS)
    valid = (kc >= qstart) & (kc < qstart + NA_COLS)
    neg = jnp.full((w, w), NEG, F32)
    tiles = []
    for ri in range(n_ri):
        t = jnp.zeros((w, w), F32)
        for j in range(n_ci):
            t = jnp.where(ci == j, rpb_ref[h * (n_ri * n_ci) + ri * n_ci + j], t)
        tiles.append(jnp.where(valid, t, neg))
    for e in range(NA_WIN):
        j0 = min(max(e - NA_ROWS // 2, 0), NA_WIN - NA_ROWS)
        for jp in range(NA_WIN // 2):
            pair = []
            for j in (2 * jp, 2 * jp + 1):
                pair.append(tiles[j - e + NA_ROWS - 1] if j0 <= j < j0 + NA_ROWS else neg)
            o_ref[0, e, :, jp * 2 * w:(jp + 1) * 2 * w] = jnp.concatenate(pair, axis=1)


def na_table(rpb):
    h, n_ri, n_ci = rpb.shape
    return pl.pallas_call(
        functools.partial(_na_table_kernel, n_ri=n_ri, n_ci=n_ci),
        grid=(h,),
        in_specs=[pl.BlockSpec(memory_space=pltpu.SMEM)],
        out_specs=pl.BlockSpec((1, NA_WIN, GRID_W, NA_WIN * GRID_W), lambda i: (i, 0, 0, 0)),
        out_shape=jax.ShapeDtypeStruct((h, NA_WIN, GRID_W, NA_WIN * GRID_W), F32),
        compiler_params=_cp(("parallel",)),
        name="na_table",
    )(rpb.reshape(-1).astype(F32))


def _na_kernel(q_ref, k_ref, v_ref, qg_ref, kg_ref, tab_ref, o_ref, qn_ref, kn_ref, *, rows):
    l = rows * GRID_W
    ch = _pick(l, (512,))

    def norm(src, g_ref, dst):
        for c in range(l // ch):
            x = src[0, c * ch:(c + 1) * ch, :].astype(F32)
            ms = jnp.mean(x * x, axis=-1, keepdims=True)
            dst[c * ch:(c + 1) * ch, :] = (x * lax.rsqrt(ms + EPS) * g_ref[...]).astype(BF16)

    norm(q_ref, qg_ref, qn_ref)
    norm(k_ref, kg_ref, kn_ref)
    scale = HEAD_DIM ** -0.5
    nq = NA_RBLK * GRID_W
    nkeys = NA_WIN * GRID_W

    def body(i, carry):
        rb = i * NA_RBLK
        kw = jnp.clip(rb - NA_ROWS // 2, 0, rows - NA_WIN)
        e0 = rb - kw
        q0 = pl.multiple_of(rb * GRID_W, nq)
        k0 = pl.multiple_of(kw * GRID_W, 256)
        q = qn_ref[pl.ds(q0, nq), :]
        kk = kn_ref[pl.ds(k0, nkeys), :]
        vv = v_ref[0, pl.ds(k0, nkeys), :]
        s = lax.dot_general(q, kk, (((1,), (1,)), ((), ())), preferred_element_type=F32) * scale
        s = s.reshape(NA_RBLK, GRID_W, nkeys) + tab_ref[0, pl.ds(e0, NA_RBLK)]
        m = jnp.max(s, axis=-1, keepdims=True)
        p = jnp.exp(s - m)
        den = jnp.sum(p, axis=-1, keepdims=True)
        o = jnp.dot(p.reshape(nq, nkeys).astype(BF16), vv, preferred_element_type=F32)
        o = o * (1.0 / den).reshape(nq, 1)
        o_ref[0, pl.ds(q0, nq), :] = o.astype(o_ref.dtype)
        return carry

    lax.fori_loop(0, rows // NA_RBLK, body, 0)


def na_attention(proj3, q_gain, k_gain, table, heads):
    b, l, _ = proj3.shape
    rows = l // GRID_W
    assert rows % NA_RBLK == 0 and rows >= NA_WIN
    blk = (1, l, HEAD_DIM)
    return pl.pallas_call(
        functools.partial(_na_kernel, rows=rows),
        grid=(b, heads),
        in_specs=[
            pl.BlockSpec(blk, lambda i, h: (i, 0, h)),
            pl.BlockSpec(blk, lambda i, h: (i, 0, heads + h)),
            pl.BlockSpec(blk, lambda i, h: (i, 0, 2 * heads + h)),
            pl.BlockSpec((1, HEAD_DIM), lambda i, h: (0, 0)),
            pl.BlockSpec((1, HEAD_DIM), lambda i, h: (0, 0)),
            pl.BlockSpec((1, NA_WIN, GRID_W, NA_WIN * GRID_W), lambda i, h: (h, 0, 0, 0)),
        ],
        out_specs=pl.BlockSpec(blk, lambda i, h: (i, 0, h)),
        out_shape=jax.ShapeDtypeStruct((b, l, heads * HEAD_DIM), BF16),
        scratch_shapes=[pltpu.VMEM((l, HEAD_DIM), BF16), pltpu.VMEM((l, HEAD_DIM), BF16)],
        compiler_params=_cp(("parallel", "parallel")),
        name="na_attention",
    )(proj3, proj3, proj3, q_gain.reshape(1, HEAD_DIM).astype(F32),
      k_gain.reshape(1, HEAD_DIM).astype(F32), table)


def _filt_kernel(w1t_ref, w1c_ref, w1s_ref, b1_ref, w2_ref, b2_ref, w3_ref, b3_ref, freq_ref, ld_ref,
                 f_ref, sum_ref, *, l, tl):
    i = pl.program_id(0)
    t = (i * tl + lax.broadcasted_iota(jnp.int32, (tl, 1), 0)).astype(F32) / l
    bands = (lax.broadcasted_iota(jnp.int32, (1, HY_BANDS), 1) + 1).astype(F32)
    ang = 2.0 * math.pi * t * bands
    dot = functools.partial(jnp.dot, precision=HIGHEST, preferred_element_type=F32)
    pre = t * w1t_ref[...] + dot(jnp.cos(ang), w1c_ref[...]) + dot(jnp.sin(ang), w1s_ref[...]) + b1_ref[...]
    freq = freq_ref[...]
    hid = jnp.sin(freq * pre)
    hid = jnp.sin(freq * (dot(hid, w2_ref[...]) + b2_ref[...]))
    f = dot(hid, w3_ref[...]) + b3_ref[...]
    f = f * jnp.exp(-jnp.exp(ld_ref[...]) * t)
    f_ref[...] = f
    part = jnp.sum(jnp.abs(f), axis=0, keepdims=True)

    @pl.when(i == 0)
    def _():
        sum_ref[...] = part

    @pl.when(i > 0)
    def _():
        sum_ref[...] += part


def _filt_combine_kernel(f_ref, sum_ref, f1_ref, f2_ref, ny_ref, *, c, tl):
    i = pl.program_id(0)
    tot = sum_ref[:, :c] + sum_ref[:, c:]
    inv = 1.0 / tot
    n = i * tl + lax.broadcasted_iota(jnp.int32, (tl, 1), 0)
    hf = f_ref[:, :c] * inv
    hb = jnp.where(n == 0, 0.0, f_ref[:, c:] * inv)
    f1 = hf + hb
    f1_ref[...] = f1.astype(f1_ref.dtype)
    f2_ref[...] = (hb - hf).astype(f2_ref.dtype)
    sign = (1 - 2 * (n & 1)).astype(F32)
    part = jnp.sum(f1 * sign, axis=0, keepdims=True)

    @pl.when(i == 0)
    def _():
        ny_ref[...] = part

    @pl.when(i > 0)
    def _():
        ny_ref[...] += part


def hyena_filters(l, w1, b1, w2, b2, w3, b3, freq, log_decay):
    hid = w2.shape[0]
    c2 = w3.shape[1]
    c = c2 // 2
    tl = _pick(l, (256,))
    row = lambda a: a.reshape(1, -1).astype(F32)
    full = lambda shape: pl.BlockSpec(shape, lambda i: (0, 0))
    f, s = pl.pallas_call(
        functools.partial(_filt_kernel, l=l, tl=tl),
        grid=(l // tl,),
        in_specs=[full((1, hid)), full((HY_BANDS, hid)), full((HY_BANDS, hid)), full((1, hid)),
                  full((hid, hid)), full((1, hid)), full((hid, c2)), full((1, c2)), full((1, hid)), full((1, c2))],
        out_specs=[pl.BlockSpec((tl, c2), lambda i: (i, 0)), full((1, c2))],
        out_shape=[jax.ShapeDtypeStruct((l, c2), F32), jax.ShapeDtypeStruct((1, c2), F32)],
        compiler_params=_cp(("arbitrary",)),
        name="hyena_filter_mlp",
    )(w1[0:1].astype(F32), w1[1:1 + HY_BANDS].astype(F32), w1[1 + HY_BANDS:].astype(F32), row(b1),
      w2.astype(F32), row(b2), w3.astype(F32), row(b3), row(freq), row(log_decay))
    return pl.pallas_call(
        functools.partial(_filt_combine_kernel, c=c, tl=tl),
        grid=(l // tl,),
        in_specs=[pl.BlockSpec((tl, c2), lambda i: (i, 0)), full((1, c2))],
        out_specs=[pl.BlockSpec((tl, c), lambda i: (i, 0)), pl.BlockSpec((tl, c), lambda i: (i, 0)), full((1, c))],
        out_shape=[jax.ShapeDtypeStruct((l, c), BF16), jax.ShapeDtypeStruct((l, c), BF16),
                   jax.ShapeDtypeStruct((1, c), F32)],
        compiler_params=_cp(("arbitrary",)),
        name="hyena_filter_combine",
    )(f, s)


def _conv_gate_kernel(z0_ref, z1_ref, z2_ref, w0_ref, w1_ref, w2_ref, b0_ref, b1_ref, b2_ref,
                      u_ref, x0_ref, *, l, ch):
    tc = z0_ref.shape[-1]

    def conv(z_ref, w_ref, b_ref, s):
        z = z_ref[0, s:s + ch, :].astype(F32)
        rid = lax.broadcasted_iota(jnp.int32, (ch, tc), 0)
        prev = z_ref[0, s - 1:s, :].astype(F32) if s > 0 else jnp.zeros((1, tc), F32)
        nxt = z_ref[0, s + ch:s + ch + 1, :].astype(F32) if s + ch < l else jnp.zeros((1, tc), F32)
        up = jnp.where(rid == 0, prev, pltpu.roll(z, 1, 0))
        dn = jnp.where(rid == ch - 1, nxt, pltpu.roll(z, ch - 1, 0))
        return w_ref[0:1, :] * up + w_ref[1:2, :] * z + w_ref[2:3, :] * dn + b_ref[...]

    for s in range(0, l, ch):
        x0 = conv(z0_ref, w0_ref, b0_ref, s)
        x1 = conv(z1_ref, w1_ref, b1_ref, s)
        v = conv(z2_ref, w2_ref, b2_ref, s)
        u_ref[0, s:s + ch, :] = (v * x1).astype(u_ref.dtype)
        x0_ref[0, s:s + ch, :] = x0.astype(x0_ref.dtype)


def hyena_conv_gate(proj3, conv_w, conv_b, base, c):
    b, l, _ = proj3.shape
    tc = _pick(math.gcd(base, c), (256, 128))
    ch = _pick(l, (512,))
    nb0, nbc = base // tc, c // tc
    zspec = lambda part: pl.BlockSpec((1, l, tc), lambda i, j: (i, 0, nb0 + part * nbc + j))
    wspec = lambda part: pl.BlockSpec((3, tc), lambda i, j: (0, part * nbc + j))
    bspec = lambda part: pl.BlockSpec((1, tc), lambda i, j: (0, part * nbc + j))
    ospec = pl.BlockSpec((1, l, tc), lambda i, j: (i, 0, j))
    cw = conv_w.astype(F32)
    cb = conv_b.reshape(1, -1).astype(F32)
    return pl.pallas_call(
        functools.partial(_conv_gate_kernel, l=l, ch=ch),
        grid=(b, nbc),
        in_specs=[zspec(0), zspec(1), zspec(2), wspec(0), wspec(1), wspec(2), bspec(0), bspec(1), bspec(2)],
        out_specs=[ospec, ospec],
        out_shape=[jax.ShapeDtypeStruct((b, l, c), BF16), jax.ShapeDtypeStruct((b, l, c), BF16)],
        compiler_params=_cp(("parallel", "parallel")),
        name="hyena_conv_gate",
    )(proj3, proj3, proj3, cw, cw, cw, cb, cb, cb)


def dft_tables(l):
    k = lax.broadcasted_iota(jnp.int32, (l, l), 0)
    t = lax.broadcasted_iota(jnp.int32, (l, l), 1)
    ang = ((k * t) % (2 * l)).astype(F32) * (math.pi / l)
    return jnp.cos(ang).astype(BF16), jnp.sin(ang).astype(BF16)


def _dft_fwd_kernel(c_ref, s_ref, u_ref, tre_ref, tim_ref, z1_ref, z2_ref, ny_ref, *, l, tk):
    kb = pl.program_id(2)
    u = u_ref[0]
    p = jnp.dot(c_ref[...], u, preferred_element_type=F32)
    q = jnp.dot(s_ref[...], u, preferred_element_type=F32)
    kidx = kb * tk + lax.broadcasted_iota(jnp.int32, (tk, 1), 0)
    sc = jnp.where(kidx == 0, 0.5 / l, 1.0 / l)
    tre = tre_ref[...] * sc
    tim = tim_ref[...] * sc
    z1_ref[0] = (p * tre + q * tim).astype(z1_ref.dtype)
    z2_ref[0] = (q * tre - p * tim).astype(z2_ref.dtype)

    @pl.when(kb == 0)
    def _():
        ch = _pick(l, (512,))
        acc = jnp.zeros((1, u_ref.shape[-1]), F32)
        for s in range(0, l, ch):
            n = s + lax.broadcasted_iota(jnp.int32, (ch, 1), 0)
            sign = (1 - 2 * (n & 1)).astype(F32)
            acc = acc + jnp.sum(u_ref[0, s:s + ch, :].astype(F32) * sign, axis=0, keepdims=True)
        ny_ref[0] = acc


def _dft_inv_kernel(c_ref, s_ref, z1_ref, z2_ref, ny_ref, tny_ref, u_ref, x0_ref, hb_ref, o_ref, *, l, tt):
    tb = pl.program_id(2)
    y = jnp.dot(c_ref[...], z1_ref[0], preferred_element_type=F32)
    y = y + jnp.dot(s_ref[...], z2_ref[0], preferred_element_type=F32)
    tidx = tb * tt + lax.broadcasted_iota(jnp.int32, (tt, 1), 0)
    sign = (1 - 2 * (tidx & 1)).astype(F32)
    y = y + sign * (ny_ref[0] * tny_ref[...] * (0.5 / l))
    y = y + u_ref[0].astype(F32) * hb_ref[...]
    o_ref[0] = (y * x0_ref[0].astype(F32)).astype(o_ref.dtype)


def hyena_long_conv(u, x0, cmat, smat, tre, tim, tny, hy_bias):
    b, l, c = u.shape
    tc = _pick(c, (512, 256, 128))
    tk = _pick(l, (512,))
    z1, z2, ny = pl.pallas_call(
        functools.partial(_dft_fwd_kernel, l=l, tk=tk),
        grid=(b, c // tc, l // tk),
        in_specs=[
            pl.BlockSpec((tk, l), lambda i, j, kb: (kb, 0)),
            pl.BlockSpec((tk, l), lambda i, j, kb: (kb, 0)),
            pl.BlockSpec((1, l, tc), lambda i, j, kb: (i, 0, j)),
            pl.BlockSpec((tk, tc), lambda i, j, kb: (kb, j)),
            pl.BlockSpec((tk, tc), lambda i, j, kb: (kb, j)),
        ],
        out_specs=[
            pl.BlockSpec((1, tk, tc), lambda i, j, kb: (i, kb, j)),
            pl.BlockSpec((1, tk, tc), lambda i, j, kb: (i, kb, j)),
            pl.BlockSpec((1, 1, tc), lambda i, j, kb: (i, 0, j)),
        ],
        out_shape=[jax.ShapeDtypeStruct((b, l, c), BF16), jax.ShapeDtypeStruct((b, l, c), BF16),
                   jax.ShapeDtypeStruct((b, 1, c), F32)],
        compiler_params=_cp(("parallel", "parallel", "arbitrary")),
        name="hyena_dft_fwd",
    )(cmat, smat, u, tre, tim)
    tt = tk
    return pl.pallas_call(
        functools.partial(_dft_inv_kernel, l=l, tt=tt),
        grid=(b, c // tc, l // tt),
        in_specs=[
            pl.BlockSpec((tt, l), lambda i, j, tb: (tb, 0)),
            pl.BlockSpec((tt, l), lambda i, j, tb: (tb, 0)),
            pl.BlockSpec((1, l, tc), lambda i, j, tb: (i, 0, j)),
            pl.BlockSpec((1, l, tc), lambda i, j, tb: (i, 0, j)),
            pl.BlockSpec((1, 1, tc), lambda i, j, tb: (i, 0, j)),
            pl.BlockSpec((1, tc), lambda i, j, tb: (0, j)),
            pl.BlockSpec((1, tt, tc), lambda i, j, tb: (i, tb, j)),
            pl.BlockSpec((1, tt, tc), lambda i, j, tb: (i, tb, j)),
            pl.BlockSpec((1, tc), lambda i, j, tb: (0, j)),
        ],
        out_specs=pl.BlockSpec((1, tt, tc), lambda i, j, tb: (i, tb, j)),
        out_shape=jax.ShapeDtypeStruct((b, l, c), BF16),
        compiler_params=_cp(("parallel", "parallel", "arbitrary")),
        name="hyena_dft_inv",
    )(cmat, smat, z1, z2, ny, tny, u, x0, hy_bias.reshape(1, c).astype(F32))


def _cmul(ar, ai, br, bi):
    return ar * br - ai * bi, ar * bi + ai * br


def _s5_prep_kernel(lre_ref, lim_ref, ls_ref, bre_ref, bim_ref, cre_ref, cim_ref,
                    pwr_ref, pwi_ref, wr_ref, wi_ref, clr_ref, cli_ref, k_ref):
    cg = bre_ref.shape[1]
    lr = jnp.minimum(lre_ref[...], -1e-4)
    li = lim_ref[...]
    step = jnp.exp(ls_ref[...])
    mag = jnp.exp(lr * step)
    ar = mag * jnp.cos(li * step)
    ai = mag * jnp.sin(li * step)
    den = lr * lr + li * li
    nr = ar - 1.0
    qr = (nr * lr + ai * li) / den
    qi = (ai * lr - nr * li) / den
    bbr, bbi = _cmul(qr, qi, bre_ref[...], bim_ref[...])
    cr = cre_ref[...]
    ci = cim_ref[...]
    pr = jnp.ones_like(ar)
    pi = jnp.zeros_like(ar)
    for tau in range(S5_T):
        sl = slice(tau * cg, (tau + 1) * cg)
        wr, wi = _cmul(pr, pi, bbr, bbi)
        wr_ref[:, sl, :] = wr
        wi_ref[:, sl, :] = wi
        pr, pi = _cmul(pr, pi, ar, ai)
        clr, cli = _cmul(cr, ci, pr, pi)
        clr_ref[:, sl, :] = clr
        cli_ref[:, sl, :] = cli
    qr, qi = pr, pi
    for m in range(8):
        pwr_ref[:, m:m + 1, :] = qr
        pwi_ref[:, m:m + 1, :] = qi
        qr, qi = _cmul(qr, qi, pr, pi)
    dims = (((2,), (2,)), ((0,), (0,)))
    k_ref[...] = (lax.dot_general(cr, wr_ref[...], dims, precision=HIGHEST, preferred_element_type=F32)
                  - lax.dot_general(ci, wi_ref[...], dims, precision=HIGHEST, preferred_element_type=F32))


def s5_prep(lam_re, lam_im, log_step, b_re, b_im, c_re, c_im):
    _, g, p = lam_re.shape
    cg = b_re.shape[-1]
    n = 2 * g
    tg = _pick(n, (8,))
    flat = lambda a: a.reshape(n, 1, p).astype(F32)
    bt = lambda a: a.reshape(n, p, cg).transpose(0, 2, 1).astype(F32)
    ct = lambda a: a.reshape(n, cg, p).astype(F32)
    s1 = pl.BlockSpec((tg, 1, p), lambda i: (i, 0, 0))
    s3 = pl.BlockSpec((tg, cg, p), lambda i: (i, 0, 0))
    s8 = pl.BlockSpec((tg, 8, p), lambda i: (i, 0, 0))
    st = pl.BlockSpec((tg, S5_T * cg, p), lambda i: (i, 0, 0))
    sk = pl.BlockSpec((tg, cg, S5_T * cg), lambda i: (i, 0, 0))
    big = jax.ShapeDtypeStruct((n, S5_T * cg, p), F32)
    return pl.pallas_call(
        _s5_prep_kernel,
        grid=(n // tg,),
        in_specs=[s1, s1, pl.BlockSpec((tg, 1, 1), lambda i: (i, 0, 0)), s3, s3, s3, s3],
        out_specs=[s8, s8, st, st, st, st, sk],
        out_shape=[jax.ShapeDtypeStruct((n, 8, p), F32)] * 2 + [big] * 4
                  + [jax.ShapeDtypeStruct((n, cg, S5_T * cg), F32)],
        compiler_params=_cp(("parallel",)),
        name="s5_discretise",
    )(flat(lam_re), flat(lam_im), log_step.reshape(n, 1, 1).astype(F32), bt(b_re), bt(b_im), ct(c_re), ct(c_im))


def _gelu(x):
    return 0.5 * x * (1.0 + jnp.tanh(math.sqrt(2.0 / math.pi) * (x + 0.044715 * (x * x * x))))


def _s5_kernel(u_ref, k_ref, b_ref, c_ref, tab_ref, o_ref, x_ref, *, s_sub, gt):
    nblk = u_ref.shape[1] // 8
    ns = x_ref.shape[-1] // 2
    half = ns // 2
    nsteps = (8 // s_sub).bit_length() - 1
    for g in range(gt):
        x_ref[g] = jnp.dot(u_ref[g], b_ref[g], preferred_element_type=F32)
    row = lax.broadcasted_iota(jnp.int32, (8, ns), 0)

    def scan_block(g, vr, vi, cr, ci, reverse):
        for k in range(nsteps):
            sh = s_sub << k
            shift = 8 - sh if reverse else sh
            ar, ai = tab_ref[g, 1 + k, :, :ns], tab_ref[g, 1 + k, :, ns:]
            dr, di = _cmul(ar, ai, pltpu.roll(vr, shift, 0), pltpu.roll(vi, shift, 0))
            vr, vi = vr + dr, vi + di
        if s_sub == 8:
            tr, ti = cr, ci
        else:
            src = (row < s_sub) if reverse else (row >= 8 - s_sub)
            tr = jnp.where(src, cr, 0.0)
            ti = jnp.where(src, ci, 0.0)
            for k in range(nsteps):
                tr = tr + pltpu.roll(tr, s_sub << k, 0)
                ti = ti + pltpu.roll(ti, s_sub << k, 0)
        dr, di = _cmul(tab_ref[g, 0, :, :ns], tab_ref[g, 0, :, ns:], tr, ti)
        fr, fi = vr + dr, vi + di
        if s_sub == 8:
            return tr, ti, fr, fi
        edge = (row >= 8 - s_sub) if reverse else (row < s_sub)
        shift = 8 - s_sub if reverse else s_sub
        xr = jnp.where(edge, tr, pltpu.roll(fr, shift, 0))
        xi = jnp.where(edge, ti, pltpu.roll(fi, shift, 0))
        return xr, xi, fr, fi

    def body(i, carry):
        of = pl.multiple_of(i * 8, 8)
        ob = pl.multiple_of((nblk - 1 - i) * 8, 8)
        new = []
        for g in range(gt):
            cfr, cfi, cbr, cbi = carry[4 * g:4 * g + 4]
            xr, xi, cfr, cfi = scan_block(g, x_ref[g, pl.ds(of, 8), :ns], x_ref[g, pl.ds(of, 8), ns:],
                                          cfr, cfi, False)
            x_ref[g, pl.ds(of, 8), :half] = xr[:, :half]
            x_ref[g, pl.ds(of, 8), ns:ns + half] = xi[:, :half]
            xr, xi, cbr, cbi = scan_block(g, x_ref[g, pl.ds(ob, 8), :ns], x_ref[g, pl.ds(ob, 8), ns:],
                                          cbr, cbi, True)
            x_ref[g, pl.ds(ob, 8), half:ns] = xr[:, half:]
            x_ref[g, pl.ds(ob, 8), ns + half:] = xi[:, half:]
            new += [cfr, cfi, cbr, cbi]
        return tuple(new)

    zero = jnp.zeros((8, ns), F32)
    lax.fori_loop(0, nblk, body, (zero,) * (4 * gt))
    for g in range(gt):
        y = jnp.dot(u_ref[g], k_ref[g], preferred_element_type=F32)
        y = y + jnp.dot(x_ref[g].astype(BF16), c_ref[g], preferred_element_type=F32)
        o_ref[g] = _gelu(y).astype(o_ref.dtype)


def s5_weights(lam_re, lam_im, log_step, b_re, b_im, c_re, c_im, d_skip):
    _, g, p = lam_re.shape
    cg = b_re.shape[-1]
    t = S5_T
    pwr, pwi, wr, wi, clr, cli, kk = s5_prep(lam_re, lam_im, log_step, b_re, b_im, c_re, c_im)
    w4 = lambda a: a.reshape(2, g, t, cg, p)
    wr, wi, clr, cli = w4(wr), w4(wi), w4(clr), w4(cli)
    bm = [wr[0, :, ::-1], wr[1], wi[0, :, ::-1], wi[1]]
    bmat = jnp.concatenate([a.reshape(g, t * cg, p) for a in bm], axis=-1).astype(BF16)
    to_rows = lambda a: a.transpose(0, 3, 1, 2).reshape(g, p, t * cg)
    cm = [to_rows(clr[0]), to_rows(clr[1, :, ::-1]), -to_rows(cli[0]), -to_rows(cli[1, :, ::-1])]
    cmat = jnp.concatenate(cm, axis=1).astype(BF16)
    kk = kk.reshape(2, g, cg, t, cg)
    ts = jnp.arange(t)
    lag = ts[None, :] - ts[:, None]
    kf = jnp.where((lag >= 0)[None, None, :, :, None], kk[0][:, :, jnp.clip(lag, 0, t - 1), :], 0.0)
    kb = jnp.where((lag <= 0)[None, None, :, :, None], kk[1][:, :, jnp.clip(-lag, 0, t - 1), :], 0.0)
    skip = (d_skip.astype(F32).reshape(g, cg)[:, :, None, None, None]
            * jnp.eye(t, dtype=F32)[None, None, :, :, None] * jnp.eye(cg, dtype=F32)[None, :, None, None, :])
    kmat = (kf + kb + skip).transpose(0, 2, 4, 3, 1).reshape(g, t * cg, t * cg).astype(BF16)
    return kmat, bmat, cmat, (pwr.reshape(2, g, 8, p), pwi.reshape(2, g, 8, p))


def s5_scan_tables(pw, s_sub):
    pwr, pwi = pw
    nsteps = (8 // s_sub).bit_length() - 1
    rows = jnp.arange(8)
    tabs = []

    def entry(fwd_idx, bwd_idx, fwd_mask, bwd_mask):
        cols = [pwr[0][:, fwd_idx] * fwd_mask[None, :, None], pwr[1][:, bwd_idx] * bwd_mask[None, :, None],
                pwi[0][:, fwd_idx] * fwd_mask[None, :, None], pwi[1][:, bwd_idx] * bwd_mask[None, :, None]]
        return jnp.concatenate(cols, axis=-1)

    ones = jnp.ones((8,), F32)
    tabs.append(entry(rows // s_sub, (7 - rows) // s_sub, ones, ones))
    for k in range(nsteps):
        sh = s_sub << k
        idx = jnp.full((8,), (1 << k) - 1)
        tabs.append(entry(idx, idx, (rows >= sh).astype(F32), (rows < 8 - sh).astype(F32)))
    return jnp.stack(tabs, axis=1)


def s5_mixer_core(xn, s5w, gt=4):
    b, l, d = xn.shape
    kmat, bmat, cmat, pw = s5w
    g = kmat.shape[0]
    cg = d // g
    t = S5_T
    nc = l // t
    s_sub = 1 << (b - 1).bit_length()
    assert s_sub <= 8 and (nc * s_sub) % 8 == 0 and g % gt == 0
    tab = s5_scan_tables(pw, s_sub)
    u = xn.reshape(b, nc, t, g, cg).transpose(3, 1, 0, 2, 4)
    u = jnp.pad(u, ((0, 0), (0, 0), (0, s_sub - b), (0, 0), (0, 0))).reshape(g, nc * s_sub, t * cg)
    r = nc * s_sub
    w = t * cg
    ns4 = bmat.shape[-1]
    gspec = lambda shape: pl.BlockSpec((gt,) + shape, lambda i: (i,) + (0,) * len(shape))
    y = pl.pallas_call(
        functools.partial(_s5_kernel, s_sub=s_sub, gt=gt),
        grid=(g // gt,),
        in_specs=[gspec((r, w)), gspec((w, w)), gspec((w, ns4)), gspec((ns4, w)), gspec((tab.shape[1], 8, ns4))],
        out_specs=gspec((r, w)),
        out_shape=jax.ShapeDtypeStruct((g, r, w), BF16),
        scratch_shapes=[pltpu.VMEM((gt, r, ns4), F32)],
        compiler_params=_cp(("parallel",)),
        name="s5_chunked",
    )(u, kmat, bmat, cmat, tab)
    y = y.reshape(g, nc, s_sub, t, cg)[:, :, :b].transpose(2, 1, 3, 0, 4)
    return y.reshape(b * l, d)


def _xattn_kernel(q_ref, k_ref, v_ref, qg_ref, kg_ref, o_ref, *, heads):
    scale = HEAD_DIM ** -0.5

    def hnorm(x, g_ref):
        ms = jnp.mean(x * x, axis=-1, keepdims=True)
        return (x * lax.rsqrt(ms + EPS) * g_ref[...]).astype(BF16)

    for h in range(heads):
        sl = slice(h * HEAD_DIM, (h + 1) * HEAD_DIM)
        q = hnorm(q_ref[0, :, sl], qg_ref)
        k = hnorm(k_ref[0, :, sl], kg_ref)
        s = lax.dot_general(q, k, (((1,), (1,)), ((), ())), preferred_element_type=F32) * scale
        m = jnp.max(s, axis=-1, keepdims=True)
        p = jnp.exp(s - m)
        den = jnp.sum(p, axis=-1, keepdims=True)
        o = jnp.dot(p.astype(BF16), v_ref[0, :, sl].astype(BF16), preferred_element_type=F32)
        o_ref[0, :, sl] = (o * (1.0 / den)).astype(o_ref.dtype)


def cross_attention_core(q, k, v, q_gain, k_gain):
    b, l, xw = q.shape
    mm = k.shape[1]
    tq = _pick(l, (512, 256, 128))
    g = lambda a: a.reshape(1, HEAD_DIM).astype(F32)
    return pl.pallas_call(
        functools.partial(_xattn_kernel, heads=xw // HEAD_DIM),
        grid=(b, l // tq),
        in_specs=[
            pl.BlockSpec((1, tq, xw), lambda i, j: (i, j, 0)),
            pl.BlockSpec((1, mm, xw), lambda i, j: (i, 0, 0)),
            pl.BlockSpec((1, mm, xw), lambda i, j: (i, 0, 0)),
            pl.BlockSpec((1, HEAD_DIM), lambda i, j: (0, 0)),
            pl.BlockSpec((1, HEAD_DIM), lambda i, j: (0, 0)),
        ],
        out_specs=pl.BlockSpec((1, tq, xw), lambda i, j: (i, j, 0)),
        out_shape=jax.ShapeDtypeStruct((b, l, xw), BF16),
        compiler_params=_cp(("parallel", "parallel")),
        name="cross_attention",
    )(q, k, v, g(q_gain), g(k_gain))


def kernel(x_prompt, x_sample, mem_prompt, mem_sample, g_mix, g_cross, g_mem, g_ffn, w_in, na_q_gain, na_k_gain, na_rpb, hy_conv_w, hy_conv_b, hy_f_w1, hy_f_b1, hy_f_w2, hy_f_b2, hy_f_w3, hy_f_b3, hy_f_freq, hy_log_decay, hy_bias, w_out, s5_lam_re, s5_lam_im, s5_log_step, s5_b_re, s5_b_im, s5_c_re, s5_c_im, s5_d, w_glu, x_wq, x_wk, x_wv, x_wo, x_q_gain, x_k_gain, w_ffn_gate, w_ffn_up, w_ffn_down):
    depth, d = g_mix.shape
    heads = na_rpb.shape[1]
    naw = heads * HEAD_DIM
    hyw = hy_bias.shape[-1]
    bf = lambda w: w.astype(BF16)

    layers = []
    for layer in range(depth):
        i = layer // 2
        p = {}
        if layer % 2 == 0:
            p["w_in"] = bf(w_in[i])
            p["w_out"] = bf(w_out[i])
            p["table"] = na_table(na_rpb[i])
            p["filters"] = {}
        else:
            p["s5w"] = s5_weights(s5_lam_re[i], s5_lam_im[i], s5_log_step[i], s5_b_re[i], s5_b_im[i],
                                  s5_c_re[i], s5_c_im[i], s5_d[i])
            p["w_val"] = bf(w_glu[i][:, :d])
            p["w_gate"] = bf(w_glu[i][:, d:])
        p["wq"], p["wk"], p["wv"], p["wo"] = bf(x_wq[layer]), bf(x_wk[layer]), bf(x_wv[layer]), bf(x_wo[layer])
        p["ffn"] = (bf(w_ffn_gate[layer]), bf(w_ffn_up[layer]), bf(w_ffn_down[layer]))
        layers.append(p)
    tables = {}

    def filters_for(layer, l):
        p = layers[layer]
        if l not in p["filters"]:
            i = layer // 2
            if l not in tables:
                tables[l] = dft_tables(l)
            cmat, smat = tables[l]
            f1, f2, ny = hyena_filters(l, hy_f_w1[i], hy_f_b1[i], hy_f_w2[i], hy_f_b2[i], hy_f_w3[i],
                                       hy_f_b3[i], hy_f_freq[i], hy_log_decay[i].reshape(-1))
            tre = matmul(cmat, [f1], out_dtype=F32)
            tim = matmul(smat, [f2], out_dtype=F32)
            p["filters"][l] = (cmat, smat, tre, tim, ny)
        return p["filters"][l]

    def run(x, mem):
        b, l, _ = x.shape
        nm = mem.shape[1]
        x = x.reshape(b * l, d)
        mem2 = mem.reshape(b * nm, d)
        for layer in range(depth):
            i = layer // 2
            p = layers[layer]
            if layer % 2 == 0:
                xn = rmsnorm(x, g_mix[layer], BF16)
                proj = matmul(xn, [p["w_in"]], out_dtype=BF16)
                proj3 = proj.reshape(b, l, -1)
                y_a = na_attention(proj3, na_q_gain[i], na_k_gain[i], p["table"], heads)
                cmat, smat, tre, tim, tny = filters_for(layer, l)
                u, x0 = hyena_conv_gate(proj3, hy_conv_w[i], hy_conv_b[i], 3 * naw, hyw)
                y_b = hyena_long_conv(u, x0, cmat, smat, tre, tim, tny, hy_bias[i])
                ymix = jnp.concatenate([y_a, y_b], axis=-1).reshape(b * l, naw + hyw)
                x = matmul(ymix, [p["w_out"]], epi="res", res=x)
            else:
                xn = rmsnorm(x, g_mix[layer], BF16)
                y = s5_mixer_core(xn.reshape(b, l, d), p["s5w"])
                x = matmul(y, [p["w_val"], p["w_gate"]], epi="glu_res", res=x)
            xn = rmsnorm(x, g_cross[layer], BF16)
            mn = rmsnorm(mem2, g_mem[layer], BF16)
            q = matmul(xn, [p["wq"]], out_dtype=F32)
            k = matmul(mn, [p["wk"]], out_dtype=F32)
            v = matmul(mn, [p["wv"]], out_dtype=F32)
            xw = q.shape[-1]
            o = cross_attention_core(q.reshape(b, l, xw), k.reshape(b, nm, xw), v.reshape(b, nm, xw),
                                     x_q_gain[layer], x_k_gain[layer])
            x = matmul(o.reshape(b * l, xw), [p["wo"]], epi="res", res=x)
            xn = rmsnorm(x, g_ffn[layer], BF16)
            wg, wu, wd = p["ffn"]
            hdn = matmul(xn, [wg, wu], epi="swiglu", out_dtype=BF16)
            x = matmul(hdn, [wd], epi="res", res=x)
        return x.reshape(b, l, d)

    return run(x_prompt, mem_prompt), run(x_sample, mem_sample)
```

```python
import functools
import math

import jax
import jax.numpy as jnp
from jax import lax
from jax.experimental import pallas as pl
from jax.experimental.pallas import tpu as pltpu

F32 = jnp.float32
BF16 = jnp.bfloat16
EPS = 1e-6
HEAD_DIM = 128
GRID_W = 64
NA_ROWS = 8
NA_COLS = 16
NA_RBLK = 8
NA_WIN = 16
HY_BANDS = 16
S5_GROUP = 16
S5_T = 16
NEG = -1e30
HIGHEST = lax.Precision.HIGHEST
VMEM_LIMIT = 56 * 1024 * 1024


def _cp(sem, vmem=VMEM_LIMIT):
    return pltpu.CompilerParams(dimension_semantics=sem, vmem_limit_bytes=vmem)


def _pick(n, cands):
    for c in cands:
        if c <= n and n % c == 0:
            return c
    return n


def _sigmoid(x):
    return 1.0 / (1.0 + jnp.exp(-x))


def _rmsnorm_kernel(x_ref, g_ref, o_ref):
    x = x_ref[...].astype(F32)
    ms = jnp.mean(x * x, axis=-1, keepdims=True)
    o_ref[...] = (x * lax.rsqrt(ms + EPS) * g_ref[...]).astype(o_ref.dtype)


def rmsnorm(x, g, out_dtype):
    m, d = x.shape
    tm = _pick(m, (256, 128, 64, 32, 16, 8))
    return pl.pallas_call(
        _rmsnorm_kernel,
        grid=(m // tm,),
        in_specs=[pl.BlockSpec((tm, d), lambda i: (i, 0)), pl.BlockSpec((1, d), lambda i: (0, 0))],
        out_specs=pl.BlockSpec((tm, d), lambda i: (i, 0)),
        out_shape=jax.ShapeDtypeStruct((m, d), out_dtype),
        compiler_params=_cp(("parallel",)),
        name="rmsnorm",
    )(x, g.reshape(1, d).astype(F32))


def _mm_kernel(*refs, na, nb, epi):
    a_refs = refs[:na]
    b_refs = refs[na:na + na * nb]
    res_ref = refs[na + na * nb] if epi in ("res", "glu_res") else None
    o_ref = refs[-1]
    avals = [a[...] for a in a_refs]
    parts = []
    for q in range(nb):
        acc = None
        for p, a in enumerate(avals):
            d = jnp.dot(a, b_refs[q * na + p][...], preferred_element_type=F32)
            acc = d if acc is None else acc + d
        parts.append(acc)
    if epi == "plain":
        out = parts[0]
    elif epi == "res":
        out = res_ref[...] + parts[0]
    elif epi == "swiglu":
        out = parts[0] * _sigmoid(parts[0]) * parts[1]
    else:
        out = res_ref[...] + parts[0] * _sigmoid(parts[1])
    o_ref[...] = out.astype(o_ref.dtype)


def matmul(a_parts, bs, n, *, epi="plain", res=None, out_dtype=F32):
    m, kp = a_parts[0].shape
    na = len(a_parts)
    k = kp * na
    wide = k <= 4096
    tm = _pick(m, ((1024,) if wide else ()) + (512, 256, 128, 64, 32, 16, 8))
    tn = _pick(n, ((512,) if wide else ()) + (256, 128))
    in_specs = [pl.BlockSpec((tm, kp), lambda i, j: (i, 0)) for _ in a_parts]
    args = list(a_parts)
    for b, off in bs:
        assert off % tn == 0 and b.shape[0] == k
        for p in range(na):
            in_specs.append(pl.BlockSpec((kp, tn), functools.partial(lambda i, j, p, ob: (p, ob + j), p=p, ob=off // tn)))
            args.append(b)
    if res is not None:
        in_specs.append(pl.BlockSpec((tm, tn), lambda i, j: (i, j)))
        args.append(res)
    return pl.pallas_call(
        functools.partial(_mm_kernel, na=na, nb=len(bs), epi=epi),
        grid=(m // tm, n // tn),
        in_specs=in_specs,
        out_specs=pl.BlockSpec((tm, tn), lambda i, j: (i, j)),
        out_shape=jax.ShapeDtypeStruct((m, n), out_dtype),
        compiler_params=_cp(("parallel", "parallel")),
        name="mm_" + epi,
    )(*args)


def _na_table_kernel(rpb_ref, o_ref, *, n_ri, n_ci):
    h = pl.program_id(0)
    w = GRID_W
    qc = lax.broadcasted_iota(jnp.int32, (w, w), 0)
    kc = lax.broadcasted_iota(jnp.int32, (w, w), 1)
    ci = kc - qc + (NA_COLS - 1)
    qstart = jnp.clip(qc - NA_COLS // 2, 0, w - NA_COLS)
    valid = (kc >= qstart) & (kc < qstart + NA_COLS)
    neg = jnp.full((w, w), NEG, F32)
    tiles = []
    for ri in range(n_ri):
        t = jnp.zeros((w, w), F32)
        for j in range(n_ci):
            t = jnp.where(ci == j, rpb_ref[h * (n_ri * n_ci) + ri * n_ci + j], t)
        tiles.append(jnp.where(valid, t, neg))
    for e in range(NA_WIN):
        j0 = min(max(e - NA_ROWS // 2, 0), NA_WIN - NA_ROWS)
        for jp in range(NA_WIN // 2):
            pair = []
            for j in (2 * jp, 2 * jp + 1):
                pair.append(tiles[j - e + NA_ROWS - 1] if j0 <= j < j0 + NA_ROWS else neg)
            o_ref[0, e, :, jp * 2 * w:(jp + 1) * 2 * w] = jnp.concatenate(pair, axis=1)


def na_table(rpb):
    h, n_ri, n_ci = rpb.shape
    return pl.pallas_call(
        functools.partial(_na_table_kernel, n_ri=n_ri, n_ci=n_ci),
        grid=(h,),
        in_specs=[pl.BlockSpec(memory_space=pltpu.SMEM)],
        out_specs=pl.BlockSpec((1, NA_WIN, GRID_W, NA_WIN * GRID_W), lambda i: (i, 0, 0, 0)),
        out_shape=jax.ShapeDtypeStruct((h, NA_WIN, GRID_W, NA_WIN * GRID_W), F32),
        compiler_params=_cp(("parallel",)),
        name="na_table",
    )(rpb.reshape(-1).astype(F32))


def _na_kernel(q_ref, k_ref, v_ref, qg_ref, kg_ref, tab_ref, o_ref, qn_ref, kn_ref, *, rows):
    l = rows * GRID_W
    ch = _pick(l, (512,))

    def norm(src, g_ref, dst):
        for c in range(l // ch):
            x = src[0, c * ch:(c + 1) * ch, :].astype(F32)
            ms = jnp.mean(x * x, axis=-1, keepdims=True)
            dst[c * ch:(c + 1) * ch, :] = (x * lax.rsqrt(ms + EPS) * g_ref[...]).astype(BF16)

    norm(q_ref, qg_ref, qn_ref)
    norm(k_ref, kg_ref, kn_ref)
    scale = HEAD_DIM ** -0.5
    nq = NA_RBLK * GRID_W
    nkeys = NA_WIN * GRID_W

    def body(i, carry):
        rb = i * NA_RBLK
        kw = jnp.clip(rb - NA_ROWS // 2, 0, rows - NA_WIN)
        e0 = rb - kw
        q0 = pl.multiple_of(rb * GRID_W, nq)
        k0 = pl.multiple_of(kw * GRID_W, 256)
        q = qn_ref[pl.ds(q0, nq), :]
        kk = kn_ref[pl.ds(k0, nkeys), :]
        vv = v_ref[0, pl.ds(k0, nkeys), :]
        s = lax.dot_general(q, kk, (((1,), (1,)), ((), ())), preferred_element_type=F32) * scale
        s = s.reshape(NA_RBLK, GRID_W, nkeys) + tab_ref[0, pl.ds(e0, NA_RBLK)]
        m = jnp.max(s, axis=-1, keepdims=True)
        p = jnp.exp(s - m)
        den = jnp.sum(p, axis=-1, keepdims=True)
        o = jnp.dot(p.reshape(nq, nkeys).astype(BF16), vv, preferred_element_type=F32)
        o = o * (1.0 / den).reshape(nq, 1)
        o_ref[0, pl.ds(q0, nq), :] = o.astype(o_ref.dtype)
        return carry

    lax.fori_loop(0, rows // NA_RBLK, body, 0)


def na_attention(proj3, q_gain, k_gain, table, heads):
    b, l, _ = proj3.shape
    rows = l // GRID_W
    assert rows % NA_RBLK == 0 and rows >= NA_WIN
    blk = (1, l, HEAD_DIM)
    return pl.pallas_call(
        functools.partial(_na_kernel, rows=rows),
        grid=(b, heads),
        in_specs=[
            pl.BlockSpec(blk, lambda i, h: (i, 0, h)),
            pl.BlockSpec(blk, lambda i, h: (i, 0, heads + h)),
            pl.BlockSpec(blk, lambda i, h: (i, 0, 2 * heads + h)),
            pl.BlockSpec((1, HEAD_DIM), lambda i, h: (0, 0)),
            pl.BlockSpec((1, HEAD_DIM), lambda i, h: (0, 0)),
            pl.BlockSpec((1, NA_WIN, GRID_W, NA_WIN * GRID_W), lambda i, h: (h, 0, 0, 0)),
        ],
        out_specs=pl.BlockSpec(blk, lambda i, h: (i, 0, h)),
        out_shape=jax.ShapeDtypeStruct((b, l, heads * HEAD_DIM), BF16),
        scratch_shapes=[pltpu.VMEM((l, HEAD_DIM), BF16), pltpu.VMEM((l, HEAD_DIM), BF16)],
        compiler_params=_cp(("parallel", "parallel")),
        name="na_attention",
    )(proj3, proj3, proj3, q_gain.reshape(1, HEAD_DIM).astype(F32),
      k_gain.reshape(1, HEAD_DIM).astype(F32), table)


def _filt_kernel(w1t_ref, w1c_ref, w1s_ref, b1_ref, w2_ref, b2_ref, w3_ref, b3_ref, freq_ref, ld_ref,
                 f_ref, sum_ref, *, l, tl):
    i = pl.program_id(0)
    t = (i * tl + lax.broadcasted_iota(jnp.int32, (tl, 1), 0)).astype(F32) / l
    bands = (lax.broadcasted_iota(jnp.int32, (1, HY_BANDS), 1) + 1).astype(F32)
    ang = 2.0 * math.pi * t * bands
    dot = functools.partial(jnp.dot, precision=HIGHEST, preferred_element_type=F32)
    pre = t * w1t_ref[...] + dot(jnp.cos(ang), w1c_ref[...]) + dot(jnp.sin(ang), w1s_ref[...]) + b1_ref[...]
    freq = freq_ref[...]
    hid = jnp.sin(freq * pre)
    hid = jnp.sin(freq * (dot(hid, w2_ref[...]) + b2_ref[...]))
    f = dot(hid, w3_ref[...]) + b3_ref[...]
    f = f * jnp.exp(-jnp.exp(ld_ref[...]) * t)
    f_ref[...] = f
    part = jnp.sum(jnp.abs(f), axis=0, keepdims=True)

    @pl.when(i == 0)
    def _():
        sum_ref[...] = part

    @pl.when(i > 0)
    def _():
        sum_ref[...] += part


def _filt_combine_kernel(f_ref, sum_ref, f1_ref, f2_ref, ny_ref, *, c, tl):
    i = pl.program_id(0)
    tot = sum_ref[:, :c] + sum_ref[:, c:]
    inv = 1.0 / tot
    n = i * tl + lax.broadcasted_iota(jnp.int32, (tl, 1), 0)
    hf = f_ref[:, :c] * inv
    hb = jnp.where(n == 0, 0.0, f_ref[:, c:] * inv)
    f1 = hf + hb
    f1_ref[...] = f1.astype(f1_ref.dtype)
    f2_ref[...] = (hb - hf).astype(f2_ref.dtype)
    sign = (1 - 2 * (n & 1)).astype(F32)
    part = jnp.sum(f1 * sign, axis=0, keepdims=True)

    @pl.when(i == 0)
    def _():
        ny_ref[...] = part

    @pl.when(i > 0)
    def _():
        ny_ref[...] += part


def hyena_filters(l, w1, b1, w2, b2, w3, b3, freq, log_decay):
    hid = w2.shape[0]
    c2 = w3.shape[1]
    c = c2 // 2
    tl = _pick(l, (256,))
    row = lambda a: a.reshape(1, -1).astype(F32)
    full = lambda shape: pl.BlockSpec(shape, lambda i: (0, 0))
    f, s = pl.pallas_call(
        functools.partial(_filt_kernel, l=l, tl=tl),
        grid=(l // tl,),
        in_specs=[full((1, hid)), full((HY_BANDS, hid)), full((HY_BANDS, hid)), full((1, hid)),
                  full((hid, hid)), full((1, hid)), full((hid, c2)), full((1, c2)), full((1, hid)), full((1, c2))],
        out_specs=[pl.BlockSpec((tl, c2), lambda i: (i, 0)), full((1, c2))],
        out_shape=[jax.ShapeDtypeStruct((l, c2), F32), jax.ShapeDtypeStruct((1, c2), F32)],
        compiler_params=_cp(("arbitrary",)),
        name="hyena_filter_mlp",
    )(w1[0:1].astype(F32), w1[1:1 + HY_BANDS].astype(F32), w1[1 + HY_BANDS:].astype(F32), row(b1),
      w2.astype(F32), row(b2), w3.astype(F32), row(b3), row(freq), row(log_decay))
    return pl.pallas_call(
        functools.partial(_filt_combine_kernel, c=c, tl=tl),
        grid=(l // tl,),
        in_specs=[pl.BlockSpec((tl, c2), lambda i: (i, 0)), full((1, c2))],
        out_specs=[pl.BlockSpec((tl, c), lambda i: (i, 0)), pl.BlockSpec((tl, c), lambda i: (i, 0)), full((1, c))],
        out_shape=[jax.ShapeDtypeStruct((l, c), BF16), jax.ShapeDtypeStruct((l, c), BF16),
                   jax.ShapeDtypeStruct((1, c), F32)],
        compiler_params=_cp(("arbitrary",)),
        name="hyena_filter_combine",
    )(f, s)


def _conv_gate_kernel(z0_ref, z1_ref, z2_ref, w0_ref, w1_ref, w2_ref, b0_ref, b1_ref, b2_ref,
                      u_ref, x0_ref, *, l, ch):
    tc = z0_ref.shape[-1]

    def conv(z_ref, w_ref, b_ref, s):
        z = z_ref[0, s:s + ch, :].astype(F32)
        rid = lax.broadcasted_iota(jnp.int32, (ch, tc), 0)
        prev = z_ref[0, s - 1:s, :].astype(F32) if s > 0 else jnp.zeros((1, tc), F32)
        nxt = z_ref[0, s + ch:s + ch + 1, :].astype(F32) if s + ch < l else jnp.zeros((1, tc), F32)
        up = jnp.where(rid == 0, prev, pltpu.roll(z, 1, 0))
        dn = jnp.where(rid == ch - 1, nxt, pltpu.roll(z, ch - 1, 0))
        return w_ref[0:1, :] * up + w_ref[1:2, :] * z + w_ref[2:3, :] * dn + b_ref[...]

    for s in range(0, l, ch):
        x0 = conv(z0_ref, w0_ref, b0_ref, s)
        x1 = conv(z1_ref, w1_ref, b1_ref, s)
        v = conv(z2_ref, w2_ref, b2_ref, s)
        u_ref[0, s:s + ch, :] = (v * x1).astype(u_ref.dtype)
        x0_ref[0, s:s + ch, :] = x0.astype(x0_ref.dtype)


def hyena_conv_gate(proj3, conv_w, conv_b, base, c):
    b, l, _ = proj3.shape
    tc = _pick(math.gcd(base, c), (256, 128))
    ch = _pick(l, (512,))
    nb0, nbc = base // tc, c // tc
    zspec = lambda part: pl.BlockSpec((1, l, tc), lambda i, j: (i, 0, nb0 + part * nbc + j))
    wspec = lambda part: pl.BlockSpec((3, tc), lambda i, j: (0, part * nbc + j))
    bspec = lambda part: pl.BlockSpec((1, tc), lambda i, j: (0, part * nbc + j))
    ospec = pl.BlockSpec((1, l, tc), lambda i, j: (i, 0, j))
    cw = conv_w.astype(F32)
    cb = conv_b.reshape(1, -1).astype(F32)
    return pl.pallas_call(
        functools.partial(_conv_gate_kernel, l=l, ch=ch),
        grid=(b, nbc),
        in_specs=[zspec(0), zspec(1), zspec(2), wspec(0), wspec(1), wspec(2), bspec(0), bspec(1), bspec(2)],
        out_specs=[ospec, ospec],
        out_shape=[jax.ShapeDtypeStruct((b, l, c), BF16), jax.ShapeDtypeStruct((b, l, c), BF16)],
        compiler_params=_cp(("parallel", "parallel")),
        name="hyena_conv_gate",
    )(proj3, proj3, proj3, cw, cw, cw, cb, cb, cb)


def dft_tables(l):
    k = lax.broadcasted_iota(jnp.int32, (l, l), 0)
    t = lax.broadcasted_iota(jnp.int32, (l, l), 1)
    ang = ((k * t) % (2 * l)).astype(F32) * (math.pi / l)
    return jnp.cos(ang).astype(BF16), jnp.sin(ang).astype(BF16)


def _dft_fwd_kernel(c_ref, s_ref, u_ref, tre_ref, tim_ref, z1_ref, z2_ref, ny_ref, *, l, tk):
    kb = pl.program_id(2)
    u = u_ref[0]
    p = jnp.dot(c_ref[...], u, preferred_element_type=F32)
    q = jnp.dot(s_ref[...], u, preferred_element_type=F32)
    kidx = kb * tk + lax.broadcasted_iota(jnp.int32, (tk, 1), 0)
    sc = jnp.where(kidx == 0, 0.5 / l, 1.0 / l)
    tre = tre_ref[...] * sc
    tim = tim_ref[...] * sc
    z1_ref[0] = (p * tre + q * tim).astype(z1_ref.dtype)
    z2_ref[0] = (q * tre - p * tim).astype(z2_ref.dtype)

    @pl.when(kb == 0)
    def _():
        ch = _pick(l, (512,))
        acc = jnp.zeros((1, u_ref.shape[-1]), F32)
        for s in range(0, l, ch):
            n = s + lax.broadcasted_iota(jnp.int32, (ch, 1), 0)
            sign = (1 - 2 * (n & 1)).astype(F32)
            acc = acc + jnp.sum(u_ref[0, s:s + ch, :].astype(F32) * sign, axis=0, keepdims=True)
        ny_ref[0] = acc


def _dft_inv_kernel(c_ref, s_ref, z1_ref, z2_ref, ny_ref, tny_ref, u_ref, x0_ref, hb_ref, o_ref, *, l, tt):
    tb = pl.program_id(2)
    y = jnp.dot(c_ref[...], z1_ref[0], preferred_element_type=F32)
    y = y + jnp.dot(s_ref[...], z2_ref[0], preferred_element_type=F32)
    tidx = tb * tt + lax.broadcasted_iota(jnp.int32, (tt, 1), 0)
    sign = (1 - 2 * (tidx & 1)).astype(F32)
    y = y + sign * (ny_ref[0] * tny_ref[...] * (0.5 / l))
    y = y + u_ref[0].astype(F32) * hb_ref[...]
    o_ref[0] = (y * x0_ref[0].astype(F32)).astype(o_ref.dtype)


def hyena_long_conv(u, x0, cmat, smat, tre, tim, tny, hy_bias):
    b, l, c = u.shape
    tc = _pick(c, (512, 256, 128))
    tk = _pick(l, (512,))
    z1, z2, ny = pl.pallas_call(
        functools.partial(_dft_fwd_kernel, l=l, tk=tk),
        grid=(b, c // tc, l // tk),
        in_specs=[
            pl.BlockSpec((tk, l), lambda i, j, kb: (kb, 0)),
            pl.BlockSpec((tk, l), lambda i, j, kb: (kb, 0)),
            pl.BlockSpec((1, l, tc), lambda i, j, kb: (i, 0, j)),
            pl.BlockSpec((tk, tc), lambda i, j, kb: (kb, j)),
            pl.BlockSpec((tk, tc), lambda i, j, kb: (kb, j)),
        ],
        out_specs=[
            pl.BlockSpec((1, tk, tc), lambda i, j, kb: (i, kb, j)),
            pl.BlockSpec((1, tk, tc), lambda i, j, kb: (i, kb, j)),
            pl.BlockSpec((1, 1, tc), lambda i, j, kb: (i, 0, j)),
        ],
        out_shape=[jax.ShapeDtypeStruct((b, l, c), BF16), jax.ShapeDtypeStruct((b, l, c), BF16),
                   jax.ShapeDtypeStruct((b, 1, c), F32)],
        compiler_params=_cp(("parallel", "parallel", "arbitrary")),
        name="hyena_dft_fwd",
    )(cmat, smat, u, tre, tim)
    tt = tk
    return pl.pallas_call(
        functools.partial(_dft_inv_kernel, l=l, tt=tt),
        grid=(b, c // tc, l // tt),
        in_specs=[
            pl.BlockSpec((tt, l), lambda i, j, tb: (tb, 0)),
            pl.BlockSpec((tt, l), lambda i, j, tb: (tb, 0)),
            pl.BlockSpec((1, l, tc), lambda i, j, tb: (i, 0, j)),
            pl.BlockSpec((1, l, tc), lambda i, j, tb: (i, 0, j)),
            pl.BlockSpec((1, 1, tc), lambda i, j, tb: (i, 0, j)),
            pl.BlockSpec((1, tc), lambda i, j, tb: (0, j)),
            pl.BlockSpec((1, tt, tc), lambda i, j, tb: (i, tb, j)),
            pl.BlockSpec((1, tt, tc), lambda i, j, tb: (i, tb, j)),
            pl.BlockSpec((1, tc), lambda i, j, tb: (0, j)),
        ],
        out_specs=pl.BlockSpec((1, tt, tc), lambda i, j, tb: (i, tb, j)),
        out_shape=jax.ShapeDtypeStruct((b, l, c), BF16),
        compiler_params=_cp(("parallel", "parallel", "arbitrary")),
        name="hyena_dft_inv",
    )(cmat, smat, z1, z2, ny, tny, u, x0, hy_bias.reshape(1, c).astype(F32))


def _cmul(ar, ai, br, bi):
    return ar * br - ai * bi, ar * bi + ai * br


def _s5_prep_kernel(lre_ref, lim_ref, ls_ref, bre_ref, bim_ref, cre_ref, cim_ref,
                    pwr_ref, pwi_ref, wr_ref, wi_ref, clr_ref, cli_ref, k_ref):
    cg = bre_ref.shape[1]
    lr = jnp.minimum(lre_ref[...], -1e-4)
    li = lim_ref[...]
    step = jnp.exp(ls_ref[...])
    mag = jnp.exp(lr * step)
    ar = mag * jnp.cos(li * step)
    ai = mag * jnp.sin(li * step)
    den = lr * lr + li * li
    nr = ar - 1.0
    qr = (nr * lr + ai * li) / den
    qi = (ai * lr - nr * li) / den
    bbr, bbi = _cmul(qr, qi, bre_ref[...], bim_ref[...])
    cr = cre_ref[...]
    ci = cim_ref[...]
    pr = jnp.ones_like(ar)
    pi = jnp.zeros_like(ar)
    for tau in range(S5_T):
        sl = slice(tau * cg, (tau + 1) * cg)
        wr, wi = _cmul(pr, pi, bbr, bbi)
        wr_ref[:, sl, :] = wr
        wi_ref[:, sl, :] = wi
        pr, pi = _cmul(pr, pi, ar, ai)
        clr, cli = _cmul(cr, ci, pr, pi)
        clr_ref[:, sl, :] = clr
        cli_ref[:, sl, :] = cli
    qr, qi = pr, pi
    for m in range(8):
        pwr_ref[:, m:m + 1, :] = qr
        pwi_ref[:, m:m + 1, :] = qi
        qr, qi = _cmul(qr, qi, pr, pi)
    dims = (((2,), (2,)), ((0,), (0,)))
    k_ref[...] = (lax.dot_general(cr, wr_ref[...], dims, precision=HIGHEST, preferred_element_type=F32)
                  - lax.dot_general(ci, wi_ref[...], dims, precision=HIGHEST, preferred_element_type=F32))


def s5_prep(lam_re, lam_im, log_step, b_re, b_im, c_re, c_im):
    _, g, p = lam_re.shape
    cg = b_re.shape[-1]
    n = 2 * g
    tg = _pick(n, (8,))
    flat = lambda a: a.reshape(n, 1, p).astype(F32)
    bt = lambda a: a.reshape(n, p, cg).transpose(0, 2, 1).astype(F32)
    ct = lambda a: a.reshape(n, cg, p).astype(F32)
    s1 = pl.BlockSpec((tg, 1, p), lambda i: (i, 0, 0))
    s3 = pl.BlockSpec((tg, cg, p), lambda i: (i, 0, 0))
    s8 = pl.BlockSpec((tg, 8, p), lambda i: (i, 0, 0))
    st = pl.BlockSpec((tg, S5_T * cg, p), lambda i: (i, 0, 0))
    sk = pl.BlockSpec((tg, cg, S5_T * cg), lambda i: (i, 0, 0))
    big = jax.ShapeDtypeStruct((n, S5_T * cg, p), F32)
    return pl.pallas_call(
        _s5_prep_kernel,
        grid=(n // tg,),
        in_specs=[s1, s1, pl.BlockSpec((tg, 1, 1), lambda i: (i, 0, 0)), s3, s3, s3, s3],
        out_specs=[s8, s8, st, st, st, st, sk],
        out_shape=[jax.ShapeDtypeStruct((n, 8, p), F32)] * 2 + [big] * 4
                  + [jax.ShapeDtypeStruct((n, cg, S5_T * cg), F32)],
        compiler_params=_cp(("parallel",)),
        name="s5_discretise",
    )(flat(lam_re), flat(lam_im), log_step.reshape(n, 1, 1).astype(F32), bt(b_re), bt(b_im), ct(c_re), ct(c_im))


def _gelu(x):
    return 0.5 * x * (1.0 + jnp.tanh(math.sqrt(2.0 / math.pi) * (x + 0.044715 * (x * x * x))))


def _s5_kernel(u_ref, k_ref, b_ref, c_ref, tab_ref, o_ref, x_ref, *, s_sub, gt):
    nblk = u_ref.shape[1] // 8
    ns = x_ref.shape[-1] // 2
    half = ns // 2
    nsteps = (8 // s_sub).bit_length() - 1
    for g in range(gt):
        x_ref[g] = jnp.dot(u_ref[g], b_ref[g], preferred_element_type=F32)
    row = lax.broadcasted_iota(jnp.int32, (8, ns), 0)

    def scan_block(g, vr, vi, cr, ci, reverse):
        for k in range(nsteps):
            sh = s_sub << k
            shift = 8 - sh if reverse else sh
            ar, ai = tab_ref[g, 1 + k, :, :ns], tab_ref[g, 1 + k, :, ns:]
            dr, di = _cmul(ar, ai, pltpu.roll(vr, shift, 0), pltpu.roll(vi, shift, 0))
            vr, vi = vr + dr, vi + di
        if s_sub == 8:
            tr, ti = cr, ci
        else:
            src = (row < s_sub) if reverse else (row >= 8 - s_sub)
            tr = jnp.where(src, cr, 0.0)
            ti = jnp.where(src, ci, 0.0)
            for k in range(nsteps):
                tr = tr + pltpu.roll(tr, s_sub << k, 0)
                ti = ti + pltpu.roll(ti, s_sub << k, 0)
        dr, di = _cmul(tab_ref[g, 0, :, :ns], tab_ref[g, 0, :, ns:], tr, ti)
        fr, fi = vr + dr, vi + di
        if s_sub == 8:
            return tr, ti, fr, fi
        edge = (row >= 8 - s_sub) if reverse else (row < s_sub)
        shift = 8 - s_sub if reverse else s_sub
        xr = jnp.where(edge, tr, pltpu.roll(fr, shift, 0))
        xi = jnp.where(edge, ti, pltpu.roll(fi, shift, 0))
        return xr, xi, fr, fi

    def body(i, carry):
        of = pl.multiple_of(i * 8, 8)
        ob = pl.multiple_of((nblk - 1 - i) * 8, 8)
        new = []
        for g in range(gt):
            cfr, cfi, cbr, cbi = carry[4 * g:4 * g + 4]
            xr, xi, cfr, cfi = scan_block(g, x_ref[g, pl.ds(of, 8), :ns], x_ref[g, pl.ds(of, 8), ns:],
                                          cfr, cfi, False)
            x_ref[g, pl.ds(of, 8), :half] = xr[:, :half]
            x_ref[g, pl.ds(of, 8), ns:ns + half] = xi[:, :half]
            xr, xi, cbr, cbi = scan_block(g, x_ref[g, pl.ds(ob, 8), :ns], x_ref[g, pl.ds(ob, 8), ns:],
                                          cbr, cbi, True)
            x_ref[g, pl.ds(ob, 8), half:ns] = xr[:, half:]
            x_ref[g, pl.ds(ob, 8), ns + half:] = xi[:, half:]
            new += [cfr, cfi, cbr, cbi]
        return tuple(new)

    zero = jnp.zeros((8, ns), F32)
    lax.fori_loop(0, nblk, body, (zero,) * (4 * gt))
    for g in range(gt):
        y = jnp.dot(u_ref[g], k_ref[g], preferred_element_type=F32)
        y = y + jnp.dot(x_ref[g].astype(BF16), c_ref[g], preferred_element_type=F32)
        o_ref[g] = _gelu(y).astype(o_ref.dtype)


def s5_weights(lam_re, lam_im, log_step, b_re, b_im, c_re, c_im, d_skip):
    _, g, p = lam_re.shape
    cg = b_re.shape[-1]
    t = S5_T
    pwr, pwi, wr, wi, clr, cli, kk = s5_prep(lam_re, lam_im, log_step, b_re, b_im, c_re, c_im)
    w4 = lambda a: a.reshape(2, g, t, cg, p)
    wr, wi, clr, cli = w4(wr), w4(wi), w4(clr), w4(cli)
    bm = [wr[0, :, ::-1], wr[1], wi[0, :, ::-1], wi[1]]
    bmat = jnp.concatenate([a.reshape(g, t * cg, p) for a in bm], axis=-1).astype(BF16)
    to_rows = lambda a: a.transpose(0, 3, 1, 2).reshape(g, p, t * cg)
    cm = [to_rows(clr[0]), to_rows(clr[1, :, ::-1]), -to_rows(cli[0]), -to_rows(cli[1, :, ::-1])]
    cmat = jnp.concatenate(cm, axis=1).astype(BF16)
    kk = kk.reshape(2, g, cg, t, cg)
    ts = jnp.arange(t)
    lag = ts[None, :] - ts[:, None]
    kf = jnp.where((lag >= 0)[None, None, :, :, None], kk[0][:, :, jnp.clip(lag, 0, t - 1), :], 0.0)
    kb = jnp.where((lag <= 0)[None, None, :, :, None], kk[1][:, :, jnp.clip(-lag, 0, t - 1), :], 0.0)
    skip = (d_skip.astype(F32).reshape(g, cg)[:, :, None, None, None]
            * jnp.eye(t, dtype=F32)[None, None, :, :, None] * jnp.eye(cg, dtype=F32)[None, :, None, None, :])
    kmat = (kf + kb + skip).transpose(0, 2, 4, 3, 1).reshape(g, t * cg, t * cg).astype(BF16)
    return kmat, bmat, cmat, (pwr.reshape(2, g, 8, p), pwi.reshape(2, g, 8, p))


def s5_scan_tables(pw, s_sub):
    pwr, pwi = pw
    nsteps = (8 // s_sub).bit_length() - 1
    rows = jnp.arange(8)
    tabs = []

    def entry(fwd_idx, bwd_idx, fwd_mask, bwd_mask):
        cols = [pwr[0][:, fwd_idx] * fwd_mask[None, :, None], pwr[1][:, bwd_idx] * bwd_mask[None, :, None],
                pwi[0][:, fwd_idx] * fwd_mask[None, :, None], pwi[1][:, bwd_idx] * bwd_mask[None, :, None]]
        return jnp.concatenate(cols, axis=-1)

    ones = jnp.ones((8,), F32)
    tabs.append(entry(rows // s_sub, (7 - rows) // s_sub, ones, ones))
    for k in range(nsteps):
        sh = s_sub << k
        idx = jnp.full((8,), (1 << k) - 1)
        tabs.append(entry(idx, idx, (rows >= sh).astype(F32), (rows < 8 - sh).astype(F32)))
    return jnp.stack(tabs, axis=1)


def _block_transpose(x):
    n = x.shape[1]
    rows = lax.broadcasted_iota(jnp.int32, x.shape, 0)
    lanes = lax.broadcasted_iota(jnp.int32, x.shape, 1)
    for k in range(4):
        s = 1 << k
        rbit = (rows >> k) & 1
        lbit = (lanes >> (4 + k)) & 1
        up = pltpu.roll(pltpu.roll(x, s, 0), n - 16 * s, 1)
        dn = pltpu.roll(pltpu.roll(x, x.shape[0] - s, 0), 16 * s, 1)
        x = jnp.where((rbit == 1) & (lbit == 0), up, jnp.where((rbit == 0) & (lbit == 1), dn, x))
    return x


def _s5_pack_kernel(x_ref, g_ref, o_ref, lo_ref, hi_ref, *, nb):
    d = x_ref.shape[1]
    hw = lo_ref.shape[1]
    w = 2 * hw
    nj = lo_ref.shape[0] // 16
    if nb is not None:
        live = pl.program_id(0) < nb
    x = x_ref[...]
    inv = lax.rsqrt(jnp.mean(x * x, axis=-1, keepdims=True) + EPS)
    for blk in range(d // w):
        sl = slice(blk * w, (blk + 1) * w)
        xn = x_ref[:, sl] * inv * g_ref[:, sl]
        if nb is not None:
            xn = jnp.where(live, xn, 0.0)
        t = _block_transpose(xn)
        lo_ref[...] = t[:, :hw]
        hi_ref[...] = t[:, hw:]
        for gl in range(16):
            o_ref[blk * 16 + gl, :, :hw] = lo_ref[pl.ds(gl, nj, stride=16), :].astype(o_ref.dtype)
            o_ref[blk * 16 + gl, :, hw:] = hi_ref[pl.ds(gl, nj, stride=16), :].astype(o_ref.dtype)


def _s5_unpack_kernel(y_ref, o_ref, lo_ref, hi_ref):
    hw = lo_ref.shape[1]
    w = 2 * hw
    nj = lo_ref.shape[0] // 16
    for blk in range(o_ref.shape[1] // w):
        for gl in range(16):
            yg = y_ref[blk * 16 + gl].astype(F32)
            lo_ref[gl * nj:(gl + 1) * nj, :] = yg[:, :hw]
            hi_ref[gl * nj:(gl + 1) * nj, :] = yg[:, hw:]
        v = jnp.concatenate(
            [jnp.concatenate([lo_ref[pl.ds(j, 16, stride=nj), :], hi_ref[pl.ds(j, 16, stride=nj), :]], axis=1)
             for j in range(nj)], axis=0)
        o_ref[:, blk * w:(blk + 1) * w] = _block_transpose(v).astype(o_ref.dtype)


def s5_mixer_core(x, gain, b, l, s5w, gt=4):
    d = x.shape[1]
    kmat, bmat, cmat, pw = s5w
    g = kmat.shape[0]
    cg = d // g
    t = S5_T
    nc = l // t
    nj = 16
    w = t * cg
    s_sub = 1 << (b - 1).bit_length()
    assert s_sub <= 8 and (nc * s_sub) % 8 == 0 and g % gt == 0 and cg == 16 and t == 16 and nc % nj == 0
    tab = s5_scan_tables(pw, s_sub)
    tok = nj * t
    lb = l // tok
    u = pl.pallas_call(
        functools.partial(_s5_pack_kernel, nb=None if s_sub == b else b),
        grid=(s_sub, lb),
        in_specs=[pl.BlockSpec((tok, d), lambda i, j: (jnp.minimum(i, b - 1) * lb + j, 0)),
                  pl.BlockSpec((1, d), lambda i, j: (0, 0))],
        out_specs=pl.BlockSpec((g, nj, w), lambda i, j: (0, j, i)),
        out_shape=jax.ShapeDtypeStruct((g, nc, s_sub * w), BF16),
        scratch_shapes=[pltpu.VMEM((nj * 16, w // 2), F32)] * 2,
        compiler_params=_cp(("parallel", "parallel")),
        name="s5_pack",
    )(x, gain.reshape(1, d).astype(F32))
    u = u.reshape(g, nc * s_sub, w)
    r = nc * s_sub
    ns4 = bmat.shape[-1]
    gspec = lambda shape: pl.BlockSpec((gt,) + shape, lambda i: (i,) + (0,) * len(shape))
    y = pl.pallas_call(
        functools.partial(_s5_kernel, s_sub=s_sub, gt=gt),
        grid=(g // gt,),
        in_specs=[gspec((r, w)), gspec((w, w)), gspec((w, ns4)), gspec((ns4, w)), gspec((tab.shape[1], 8, ns4))],
        out_specs=gspec((r, w)),
        out_shape=jax.ShapeDtypeStruct((g, r, w), BF16),
        scratch_shapes=[pltpu.VMEM((gt, r, ns4), F32)],
        compiler_params=_cp(("parallel",)),
        name="s5_chunked",
    )(u, kmat, bmat, cmat, tab)
    return pl.pallas_call(
        _s5_unpack_kernel,
        grid=(b, lb),
        in_specs=[pl.BlockSpec((g, nj, w), lambda i, j: (0, j, i))],
        out_specs=pl.BlockSpec((tok, d), lambda i, j: (i * lb + j, 0)),
        out_shape=jax.ShapeDtypeStruct((b * l, d), BF16),
        scratch_shapes=[pltpu.VMEM((nj * 16, w // 2), F32)] * 2,
        compiler_params=_cp(("parallel", "parallel")),
        name="s5_unpack",
    )(y.reshape(g, nc, s_sub * w))


def _xattn_kernel(q_ref, k_ref, v_ref, qg_ref, kg_ref, o_ref, *, heads):
    scale = HEAD_DIM ** -0.5

    def hnorm(x, g_ref):
        ms = jnp.mean(x * x, axis=-1, keepdims=True)
        return (x * lax.rsqrt(ms + EPS) * g_ref[...]).astype(BF16)

    for h in range(heads):
        sl = slice(h * HEAD_DIM, (h + 1) * HEAD_DIM)
        q = hnorm(q_ref[0, :, sl], qg_ref)
        k = hnorm(k_ref[0, :, sl], kg_ref)
        s = lax.dot_general(q, k, (((1,), (1,)), ((), ())), preferred_element_type=F32) * scale
        m = jnp.max(s, axis=-1, keepdims=True)
        p = jnp.exp(s - m)
        den = jnp.sum(p, axis=-1, keepdims=True)
        o = jnp.dot(p.astype(BF16), v_ref[0, :, sl].astype(BF16), preferred_element_type=F32)
        o_ref[0, :, sl] = (o * (1.0 / den)).astype(o_ref.dtype)


def cross_attention_core(q, k, v, q_gain, k_gain):
    b, l, xw = q.shape
    mm = k.shape[1]
    tq = _pick(l, (512, 256, 128))
    g = lambda a: a.reshape(1, HEAD_DIM).astype(F32)
    return pl.pallas_call(
        functools.partial(_xattn_kernel, heads=xw // HEAD_DIM),
        grid=(b, l // tq),
        in_specs=[
            pl.BlockSpec((1, tq, xw), lambda i, j: (i, j, 0)),
            pl.BlockSpec((1, mm, xw), lambda i, j: (i, 0, 0)),
            pl.BlockSpec((1, mm, xw), lambda i, j: (i, 0, 0)),
            pl.BlockSpec((1, HEAD_DIM), lambda i, j: (0, 0)),
            pl.BlockSpec((1, HEAD_DIM), lambda i, j: (0, 0)),
        ],
        out_specs=pl.BlockSpec((1, tq, xw), lambda i, j: (i, j, 0)),
        out_shape=jax.ShapeDtypeStruct((b, l, xw), BF16),
        compiler_params=_cp(("parallel", "parallel")),
        name="cross_attention",
    )(q, k, v, g(q_gain), g(k_gain))


def kernel(x_prompt, x_sample, mem_prompt, mem_sample, g_mix, g_cross, g_mem, g_ffn, w_in, na_q_gain, na_k_gain, na_rpb, hy_conv_w, hy_conv_b, hy_f_w1, hy_f_b1, hy_f_w2, hy_f_b2, hy_f_w3, hy_f_b3, hy_f_freq, hy_log_decay, hy_bias, w_out, s5_lam_re, s5_lam_im, s5_log_step, s5_b_re, s5_b_im, s5_c_re, s5_c_im, s5_d, w_glu, x_wq, x_wk, x_wv, x_wo, x_q_gain, x_k_gain, w_ffn_gate, w_ffn_up, w_ffn_down):
    depth, d = g_mix.shape
    heads = na_rpb.shape[1]
    naw = heads * HEAD_DIM
    hyw = hy_bias.shape[-1]
    bf = lambda w: w.astype(BF16)

    layers = []
    for layer in range(depth):
        i = layer // 2
        p = {}
        if layer % 2 == 0:
            p["w_in"] = bf(w_in[i])
            p["w_out"] = bf(w_out[i])
            p["table"] = na_table(na_rpb[i])
            p["filters"] = {}
        else:
            p["s5w"] = s5_weights(s5_lam_re[i], s5_lam_im[i], s5_log_step[i], s5_b_re[i], s5_b_im[i],
                                  s5_c_re[i], s5_c_im[i], s5_d[i])
            p["w_glu"] = bf(w_glu[i])
        p["wq"], p["wk"], p["wv"], p["wo"] = bf(x_wq[layer]), bf(x_wk[layer]), bf(x_wv[layer]), bf(x_wo[layer])
        p["ffn"] = (bf(w_ffn_gate[layer]), bf(w_ffn_up[layer]), bf(w_ffn_down[layer]))
        layers.append(p)
    tables = {}

    def filters_for(layer, l):
        p = layers[layer]
        if l not in p["filters"]:
            i = layer // 2
            if l not in tables:
                tables[l] = dft_tables(l)
            cmat, smat = tables[l]
            f1, f2, ny = hyena_filters(l, hy_f_w1[i], hy_f_b1[i], hy_f_w2[i], hy_f_b2[i], hy_f_w3[i],
                                       hy_f_b3[i], hy_f_freq[i], hy_log_decay[i].reshape(-1))
            tre = matmul([cmat], [(f1, 0)], f1.shape[1])
            tim = matmul([smat], [(f2, 0)], f2.shape[1])
            p["filters"][l] = (cmat, smat, tre, tim, ny)
        return p["filters"][l]

    def run(x, mem):
        b, l, _ = x.shape
        nm = mem.shape[1]
        x = x.reshape(b * l, d)
        mem2 = mem.reshape(b * nm, d)
        for layer in range(depth):
            i = layer // 2
            p = layers[layer]
            if layer % 2 == 0:
                xn = rmsnorm(x, g_mix[layer], BF16)
                proj = matmul([xn], [(p["w_in"], 0)], p["w_in"].shape[1], out_dtype=BF16)
                proj3 = proj.reshape(b, l, -1)
                y_a = na_attention(proj3, na_q_gain[i], na_k_gain[i], p["table"], heads)
                cmat, smat, tre, tim, tny = filters_for(layer, l)
                u, x0 = hyena_conv_gate(proj3, hy_conv_w[i], hy_conv_b[i], 3 * naw, hyw)
                y_b = hyena_long_conv(u, x0, cmat, smat, tre, tim, tny, hy_bias[i])
                y_a, y_b = y_a.reshape(b * l, naw), y_b.reshape(b * l, hyw)
                ymix = [y_a, y_b] if naw == hyw else [jnp.concatenate([y_a, y_b], axis=-1)]
                x = matmul(ymix, [(p["w_out"], 0)], d, epi="res", res=x)
            else:
                y = s5_mixer_core(x, g_mix[layer], b, l, p["s5w"])
                x = matmul([y], [(p["w_glu"], 0), (p["w_glu"], d)], d, epi="glu_res", res=x)
            xn = rmsnorm(x, g_cross[layer], BF16)
            mn = rmsnorm(mem2, g_mem[layer], BF16)
            xw = p["wq"].shape[1]
            q = matmul([xn], [(p["wq"], 0)], xw)
            k = matmul([mn], [(p["wk"], 0)], xw)
            v = matmul([mn], [(p["wv"], 0)], xw)
            o = cross_attention_core(q.reshape(b, l, xw), k.reshape(b, nm, xw), v.reshape(b, nm, xw),
                                     x_q_gain[layer], x_k_gain[layer])
            x = matmul([o.reshape(b * l, xw)], [(p["wo"], 0)], d, epi="res", res=x)
            xn = rmsnorm(x, g_ffn[layer], BF16)
            wg, wu, wd = p["ffn"]
            hdn = matmul([xn], [(wg, 0), (wu, 0)], wg.shape[1], epi="swiglu", out_dtype=BF16)
            x = matmul([hdn], [(wd, 0)], d, epi="res", res=x)
        return x.reshape(b, l, d)

    return run(x_prompt, mem_prompt), run(x_sample, mem_sample)
```

```python
import functools
import math

import jax
import jax.numpy as jnp
from jax import lax
from jax.experimental import pallas as pl
from jax.experimental.pallas import tpu as pltpu

F32 = jnp.float32
BF16 = jnp.bfloat16
EPS = 1e-6
HEAD_DIM = 128
GRID_W = 64
NA_ROWS = 8
NA_COLS = 16
NA_RBLK = 8
NA_WIN = 16
HY_BANDS = 16
S5_GROUP = 16
S5_T = 16
NEG = -1e30
HIGHEST = lax.Precision.HIGHEST
VMEM_LIMIT = 56 * 1024 * 1024


def _cp(sem, vmem=VMEM_LIMIT):
    return pltpu.CompilerParams(dimension_semantics=sem, vmem_limit_bytes=vmem)


def _pick(n, cands):
    for c in cands:
        if c <= n and n % c == 0:
            return c
    return n


def _sigmoid(x):
    return 1.0 / (1.0 + jnp.exp(-x))


def _rmsnorm_kernel(x_ref, g_ref, o_ref):
    x = x_ref[...].astype(F32)
    ms = jnp.mean(x * x, axis=-1, keepdims=True)
    o_ref[...] = (x * lax.rsqrt(ms + EPS) * g_ref[...]).astype(o_ref.dtype)


def rmsnorm(x, g, out_dtype):
    m, d = x.shape
    tm = _pick(m, (256, 128, 64, 32, 16, 8))
    return pl.pallas_call(
        _rmsnorm_kernel,
        grid=(m // tm,),
        in_specs=[pl.BlockSpec((tm, d), lambda i: (i, 0)), pl.BlockSpec((1, d), lambda i: (0, 0))],
        out_specs=pl.BlockSpec((tm, d), lambda i: (i, 0)),
        out_shape=jax.ShapeDtypeStruct((m, d), out_dtype),
        compiler_params=_cp(("parallel",)),
        name="rmsnorm",
    )(x, g.reshape(1, d).astype(F32))


def _mm_kernel(*refs, na, nb, epi, norm):
    a_refs = refs[:na]
    b_refs = refs[na:na + na * nb]
    pos = na + na * nb
    res_ref = None
    if epi in ("res", "glu_res"):
        res_ref = refs[pos]
        pos += 1
    if norm:
        g_ref, o_ref, xn_ref = refs[pos], refs[pos + 1], refs[pos + 2]

        @pl.when(pl.program_id(1) == 0)
        def _():
            rows = xn_ref.shape[0]
            ch = _pick(rows, (256,))
            for s in range(0, rows, ch):
                x = a_refs[0][s:s + ch, :]
                inv = lax.rsqrt(jnp.mean(x * x, axis=-1, keepdims=True) + EPS)
                xn_ref[s:s + ch, :] = (x * inv * g_ref[...]).astype(xn_ref.dtype)

        avals = [xn_ref[...]]
    else:
        o_ref = refs[pos]
        avals = [a[...] for a in a_refs]
    parts = []
    for q in range(nb):
        acc = None
        for p, a in enumerate(avals):
            d = jnp.dot(a, b_refs[q * na + p][...], preferred_element_type=F32)
            acc = d if acc is None else acc + d
        parts.append(acc)
    if epi == "plain":
        out = parts[0]
    elif epi == "res":
        out = res_ref[...] + parts[0]
    elif epi == "swiglu":
        out = parts[0] * _sigmoid(parts[0]) * parts[1]
    else:
        out = res_ref[...] + parts[0] * _sigmoid(parts[1])
    o_ref[...] = out.astype(o_ref.dtype)


def matmul(a_parts, bs, n, *, epi="plain", res=None, out_dtype=F32, norm_gain=None):
    m, kp = a_parts[0].shape
    na = len(a_parts)
    k = kp * na
    wide = k <= 4096
    norm = norm_gain is not None
    tm = _pick(m, ((1024,) if wide else ()) + (512, 256, 128, 64, 32, 16, 8))
    tn = _pick(n, (512, 256, 128))
    single = norm and n // tn >= 8
    if norm and not single:
        tm = _pick(m, (512, 256, 128, 64, 32, 16, 8))
    amode = dict(pipeline_mode=pl.Buffered(1)) if single else {}
    in_specs = [pl.BlockSpec((tm, kp), lambda i, j: (i, 0), **amode) for _ in a_parts]
    args = list(a_parts)
    for b, off in bs:
        assert off % tn == 0 and b.shape[0] == k
        for p in range(na):
            in_specs.append(pl.BlockSpec((kp, tn), functools.partial(lambda i, j, p, ob: (p, ob + j), p=p, ob=off // tn)))
            args.append(b)
    if res is not None:
        in_specs.append(pl.BlockSpec((tm, tn), lambda i, j: (i, j)))
        args.append(res)
    if norm:
        assert na == 1
        in_specs.append(pl.BlockSpec((1, k), lambda i, j: (0, 0)))
        args.append(norm_gain.reshape(1, k).astype(F32))
    return pl.pallas_call(
        functools.partial(_mm_kernel, na=na, nb=len(bs), epi=epi, norm=norm),
        grid=(m // tm, n // tn),
        in_specs=in_specs,
        out_specs=pl.BlockSpec((tm, tn), lambda i, j: (i, j)),
        out_shape=jax.ShapeDtypeStruct((m, n), out_dtype),
        scratch_shapes=[pltpu.VMEM((tm, k), BF16)] if norm else [],
        compiler_params=_cp(("parallel", "arbitrary" if norm else "parallel")),
        name=("mm_norm_" if norm else "mm_") + epi,
    )(*args)


def _cast_kernel(w_ref, o_ref):
    o_ref[...] = w_ref[0].astype(o_ref.dtype)


def cast_weight(w, layer):
    _, k, n = w.shape
    tk = _pick(k, (128, 64, 32, 16))
    return pl.pallas_call(
        _cast_kernel,
        grid=(k // tk,),
        in_specs=[pl.BlockSpec((1, tk, n), lambda i: (layer, i, 0))],
        out_specs=pl.BlockSpec((tk, n), lambda i: (i, 0)),
        out_shape=jax.ShapeDtypeStruct((k, n), BF16),
        compiler_params=_cp(("parallel",)),
        name="cast_weight",
    )(w)


def _na_table_kernel(rpb_ref, o_ref, *, n_ri, n_ci):
    h = pl.program_id(0)
    w = GRID_W
    qc = lax.broadcasted_iota(jnp.int32, (w, w), 0)
    kc = lax.broadcasted_iota(jnp.int32, (w, w), 1)
    ci = kc - qc + (NA_COLS - 1)
    qstart = jnp.clip(qc - NA_COLS // 2, 0, w - NA_COLS)
    valid = (kc >= qstart) & (kc < qstart + NA_COLS)
    neg = jnp.full((w, w), NEG, F32)
    tiles = []
    for ri in range(n_ri):
        t = jnp.zeros((w, w), F32)
        for j in range(n_ci):
            t = jnp.where(ci == j, rpb_ref[h * (n_ri * n_ci) + ri * n_ci + j], t)
        tiles.append(jnp.where(valid, t, neg))
    for e in range(NA_WIN):
        j0 = min(max(e - NA_ROWS // 2, 0), NA_WIN - NA_ROWS)
        for jp in range(NA_WIN // 2):
            pair = []
            for j in (2 * jp, 2 * jp + 1):
                pair.append(tiles[j - e + NA_ROWS - 1] if j0 <= j < j0 + NA_ROWS else neg)
            o_ref[0, e, :, jp * 2 * w:(jp + 1) * 2 * w] = jnp.concatenate(pair, axis=1)


def na_table(rpb):
    h, n_ri, n_ci = rpb.shape
    return pl.pallas_call(
        functools.partial(_na_table_kernel, n_ri=n_ri, n_ci=n_ci),
        grid=(h,),
        in_specs=[pl.BlockSpec(memory_space=pltpu.SMEM)],
        out_specs=pl.BlockSpec((1, NA_WIN, GRID_W, NA_WIN * GRID_W), lambda i: (i, 0, 0, 0)),
        out_shape=jax.ShapeDtypeStruct((h, NA_WIN, GRID_W, NA_WIN * GRID_W), F32),
        compiler_params=_cp(("parallel",)),
        name="na_table",
    )(rpb.reshape(-1).astype(F32))


def _na_kernel(q_ref, k_ref, v_ref, qg_ref, kg_ref, tab_ref, o_ref, qn_ref, kn_ref, *, rows):
    l = rows * GRID_W
    ch = _pick(l, (512,))

    def norm(src, g_ref, dst):
        for c in range(l // ch):
            x = src[0, c * ch:(c + 1) * ch, :].astype(F32)
            ms = jnp.mean(x * x, axis=-1, keepdims=True)
            dst[c * ch:(c + 1) * ch, :] = (x * lax.rsqrt(ms + EPS) * g_ref[...]).astype(BF16)

    norm(q_ref, qg_ref, qn_ref)
    norm(k_ref, kg_ref, kn_ref)
    scale = HEAD_DIM ** -0.5
    nq = NA_RBLK * GRID_W
    nkeys = NA_WIN * GRID_W

    def body(i, carry):
        rb = i * NA_RBLK
        kw = jnp.clip(rb - NA_ROWS // 2, 0, rows - NA_WIN)
        e0 = rb - kw
        q0 = pl.multiple_of(rb * GRID_W, nq)
        k0 = pl.multiple_of(kw * GRID_W, 256)
        q = qn_ref[pl.ds(q0, nq), :]
        kk = kn_ref[pl.ds(k0, nkeys), :]
        vv = v_ref[0, pl.ds(k0, nkeys), :]
        s = lax.dot_general(q, kk, (((1,), (1,)), ((), ())), preferred_element_type=F32) * scale
        s = s.reshape(NA_RBLK, GRID_W, nkeys) + tab_ref[0, pl.ds(e0, NA_RBLK)]
        m = jnp.max(s, axis=-1, keepdims=True)
        p = jnp.exp(s - m)
        den = jnp.sum(p, axis=-1, keepdims=True)
        o = jnp.dot(p.reshape(nq, nkeys).astype(BF16), vv, preferred_element_type=F32)
        o = o * (1.0 / den).reshape(nq, 1)
        o_ref[0, pl.ds(q0, nq), :] = o.astype(o_ref.dtype)
        return carry

    lax.fori_loop(0, rows // NA_RBLK, body, 0)


def na_attention(proj3, q_gain, k_gain, table, heads):
    b, l, _ = proj3.shape
    rows = l // GRID_W
    assert rows % NA_RBLK == 0 and rows >= NA_WIN
    blk = (1, l, HEAD_DIM)
    return pl.pallas_call(
        functools.partial(_na_kernel, rows=rows),
        grid=(b, heads),
        in_specs=[
            pl.BlockSpec(blk, lambda i, h: (i, 0, h)),
            pl.BlockSpec(blk, lambda i, h: (i, 0, heads + h)),
            pl.BlockSpec(blk, lambda i, h: (i, 0, 2 * heads + h)),
            pl.BlockSpec((1, HEAD_DIM), lambda i, h: (0, 0)),
            pl.BlockSpec((1, HEAD_DIM), lambda i, h: (0, 0)),
            pl.BlockSpec((1, NA_WIN, GRID_W, NA_WIN * GRID_W), lambda i, h: (h, 0, 0, 0)),
        ],
        out_specs=pl.BlockSpec(blk, lambda i, h: (i, 0, h)),
        out_shape=jax.ShapeDtypeStruct((b, l, heads * HEAD_DIM), BF16),
        scratch_shapes=[pltpu.VMEM((l, HEAD_DIM), BF16), pltpu.VMEM((l, HEAD_DIM), BF16)],
        compiler_params=_cp(("parallel", "parallel")),
        name="na_attention",
    )(proj3, proj3, proj3, q_gain.reshape(1, HEAD_DIM).astype(F32),
      k_gain.reshape(1, HEAD_DIM).astype(F32), table)


def _filt_kernel(w1t_ref, w1c_ref, w1s_ref, b1_ref, w2_ref, b2_ref, w3_ref, b3_ref, freq_ref, ld_ref,
                 f_ref, sum_ref, *, l, tl):
    i = pl.program_id(0)
    t = (i * tl + lax.broadcasted_iota(jnp.int32, (tl, 1), 0)).astype(F32) / l
    bands = (lax.broadcasted_iota(jnp.int32, (1, HY_BANDS), 1) + 1).astype(F32)
    ang = 2.0 * math.pi * t * bands
    dot = functools.partial(jnp.dot, precision=HIGHEST, preferred_element_type=F32)
    pre = t * w1t_ref[...] + dot(jnp.cos(ang), w1c_ref[...]) + dot(jnp.sin(ang), w1s_ref[...]) + b1_ref[...]
    freq = freq_ref[...]
    hid = jnp.sin(freq * pre)
    hid = jnp.sin(freq * (dot(hid, w2_ref[...]) + b2_ref[...]))
    f = dot(hid, w3_ref[...]) + b3_ref[...]
    f = f * jnp.exp(-jnp.exp(ld_ref[...]) * t)
    f_ref[...] = f
    part = jnp.sum(jnp.abs(f), axis=0, keepdims=True)

    @pl.when(i == 0)
    def _():
        sum_ref[...] = part

    @pl.when(i > 0)
    def _():
        sum_ref[...] += part


def _filt_combine_kernel(f_ref, sum_ref, f1_ref, f2_ref, ny_ref, *, c, tl):
    i = pl.program_id(0)
    tot = sum_ref[:, :c] + sum_ref[:, c:]
    inv = 1.0 / tot
    n = i * tl + lax.broadcasted_iota(jnp.int32, (tl, 1), 0)
    hf = f_ref[:, :c] * inv
    hb = jnp.where(n == 0, 0.0, f_ref[:, c:] * inv)
    f1 = hf + hb
    f1_ref[...] = f1.astype(f1_ref.dtype)
    f2_ref[...] = (hb - hf).astype(f2_ref.dtype)
    sign = (1 - 2 * (n & 1)).astype(F32)
    part = jnp.sum(f1 * sign, axis=0, keepdims=True)

    @pl.when(i == 0)
    def _():
        ny_ref[...] = part

    @pl.when(i > 0)
    def _():
        ny_ref[...] += part


def hyena_filters(l, w1, b1, w2, b2, w3, b3, freq, log_decay):
    hid = w2.shape[0]
    c2 = w3.shape[1]
    c = c2 // 2
    tl = _pick(l, (256,))
    row = lambda a: a.reshape(1, -1).astype(F32)
    full = lambda shape: pl.BlockSpec(shape, lambda i: (0, 0))
    f, s = pl.pallas_call(
        functools.partial(_filt_kernel, l=l, tl=tl),
        grid=(l // tl,),
        in_specs=[full((1, hid)), full((HY_BANDS, hid)), full((HY_BANDS, hid)), full((1, hid)),
                  full((hid, hid)), full((1, hid)), full((hid, c2)), full((1, c2)), full((1, hid)), full((1, c2))],
        out_specs=[pl.BlockSpec((tl, c2), lambda i: (i, 0)), full((1, c2))],
        out_shape=[jax.ShapeDtypeStruct((l, c2), F32), jax.ShapeDtypeStruct((1, c2), F32)],
        compiler_params=_cp(("arbitrary",)),
        name="hyena_filter_mlp",
    )(w1[0:1].astype(F32), w1[1:1 + HY_BANDS].astype(F32), w1[1 + HY_BANDS:].astype(F32), row(b1),
      w2.astype(F32), row(b2), w3.astype(F32), row(b3), row(freq), row(log_decay))
    return pl.pallas_call(
        functools.partial(_filt_combine_kernel, c=c, tl=tl),
        grid=(l // tl,),
        in_specs=[pl.BlockSpec((tl, c2), lambda i: (i, 0)), full((1, c2))],
        out_specs=[pl.BlockSpec((tl, c), lambda i: (i, 0)), pl.BlockSpec((tl, c), lambda i: (i, 0)), full((1, c))],
        out_shape=[jax.ShapeDtypeStruct((l, c), BF16), jax.ShapeDtypeStruct((l, c), BF16),
                   jax.ShapeDtypeStruct((1, c), F32)],
        compiler_params=_cp(("arbitrary",)),
        name="hyena_filter_combine",
    )(f, s)


def _conv_gate_kernel(z0_ref, z1_ref, z2_ref, w0_ref, w1_ref, w2_ref, b0_ref, b1_ref, b2_ref,
                      u_ref, x0_ref, *, l, ch):
    tc = z0_ref.shape[-1]

    def conv(z_ref, w_ref, b_ref, s):
        z = z_ref[0, s:s + ch, :].astype(F32)
        rid = lax.broadcasted_iota(jnp.int32, (ch, tc), 0)
        prev = z_ref[0, s - 1:s, :].astype(F32) if s > 0 else jnp.zeros((1, tc), F32)
        nxt = z_ref[0, s + ch:s + ch + 1, :].astype(F32) if s + ch < l else jnp.zeros((1, tc), F32)
        up = jnp.where(rid == 0, prev, pltpu.roll(z, 1, 0))
        dn = jnp.where(rid == ch - 1, nxt, pltpu.roll(z, ch - 1, 0))
        return w_ref[0:1, :] * up + w_ref[1:2, :] * z + w_ref[2:3, :] * dn + b_ref[...]

    for s in range(0, l, ch):
        x0 = conv(z0_ref, w0_ref, b0_ref, s)
        x1 = conv(z1_ref, w1_ref, b1_ref, s)
        v = conv(z2_ref, w2_ref, b2_ref, s)
        u_ref[0, s:s + ch, :] = (v * x1).astype(u_ref.dtype)
        x0_ref[0, s:s + ch, :] = x0.astype(x0_ref.dtype)


def hyena_conv_gate(proj3, conv_w, conv_b, base, c):
    b, l, _ = proj3.shape
    tc = _pick(math.gcd(base, c), (256, 128))
    ch = _pick(l, (512,))
    nb0, nbc = base // tc, c // tc
    zspec = lambda part: pl.BlockSpec((1, l, tc), lambda i, j: (i, 0, nb0 + part * nbc + j))
    wspec = lambda part: pl.BlockSpec((3, tc), lambda i, j: (0, part * nbc + j))
    bspec = lambda part: pl.BlockSpec((1, tc), lambda i, j: (0, part * nbc + j))
    ospec = pl.BlockSpec((1, l, tc), lambda i, j: (i, 0, j))
    cw = conv_w.astype(F32)
    cb = conv_b.reshape(1, -1).astype(F32)
    return pl.pallas_call(
        functools.partial(_conv_gate_kernel, l=l, ch=ch),
        grid=(b, nbc),
        in_specs=[zspec(0), zspec(1), zspec(2), wspec(0), wspec(1), wspec(2), bspec(0), bspec(1), bspec(2)],
        out_specs=[ospec, ospec],
        out_shape=[jax.ShapeDtypeStruct((b, l, c), BF16), jax.ShapeDtypeStruct((b, l, c), BF16)],
        compiler_params=_cp(("parallel", "parallel")),
        name="hyena_conv_gate",
    )(proj3, proj3, proj3, cw, cw, cw, cb, cb, cb)


def dft_tables(l):
    k = lax.broadcasted_iota(jnp.int32, (l, l), 0)
    t = lax.broadcasted_iota(jnp.int32, (l, l), 1)
    ang = ((k * t) % (2 * l)).astype(F32) * (math.pi / l)
    return jnp.cos(ang).astype(BF16), jnp.sin(ang).astype(BF16)


def _dft_fwd_kernel(c_ref, s_ref, u_ref, tre_ref, tim_ref, z1_ref, z2_ref, ny_ref, *, l, tk):
    kb = pl.program_id(2)
    u = u_ref[0]
    p = jnp.dot(c_ref[...], u, preferred_element_type=F32)
    q = jnp.dot(s_ref[...], u, preferred_element_type=F32)
    kidx = kb * tk + lax.broadcasted_iota(jnp.int32, (tk, 1), 0)
    sc = jnp.where(kidx == 0, 0.5 / l, 1.0 / l)
    tre = tre_ref[...] * sc
    tim = tim_ref[...] * sc
    z1_ref[0] = (p * tre + q * tim).astype(z1_ref.dtype)
    z2_ref[0] = (q * tre - p * tim).astype(z2_ref.dtype)

    @pl.when(kb == 0)
    def _():
        ch = _pick(l, (512,))
        acc = jnp.zeros((1, u_ref.shape[-1]), F32)
        for s in range(0, l, ch):
            n = s + lax.broadcasted_iota(jnp.int32, (ch, 1), 0)
            sign = (1 - 2 * (n & 1)).astype(F32)
            acc = acc + jnp.sum(u_ref[0, s:s + ch, :].astype(F32) * sign, axis=0, keepdims=True)
        ny_ref[0] = acc


def _dft_inv_kernel(c_ref, s_ref, z1_ref, z2_ref, ny_ref, tny_ref, u_ref, x0_ref, hb_ref, o_ref, *, l, tt):
    tb = pl.program_id(2)
    y = jnp.dot(c_ref[...], z1_ref[0], preferred_element_type=F32)
    y = y + jnp.dot(s_ref[...], z2_ref[0], preferred_element_type=F32)
    tidx = tb * tt + lax.broadcasted_iota(jnp.int32, (tt, 1), 0)
    sign = (1 - 2 * (tidx & 1)).astype(F32)
    y = y + sign * (ny_ref[0] * tny_ref[...] * (0.5 / l))
    y = y + u_ref[0].astype(F32) * hb_ref[...]
    o_ref[0] = (y * x0_ref[0].astype(F32)).astype(o_ref.dtype)


def hyena_long_conv(u, x0, cmat, smat, tre, tim, tny, hy_bias):
    b, l, c = u.shape
    tc = _pick(c, (512, 256, 128))
    tk = _pick(l, (512,))
    z1, z2, ny = pl.pallas_call(
        functools.partial(_dft_fwd_kernel, l=l, tk=tk),
        grid=(b, c // tc, l // tk),
        in_specs=[
            pl.BlockSpec((tk, l), lambda i, j, kb: (kb, 0)),
            pl.BlockSpec((tk, l), lambda i, j, kb: (kb, 0)),
            pl.BlockSpec((1, l, tc), lambda i, j, kb: (i, 0, j)),
            pl.BlockSpec((tk, tc), lambda i, j, kb: (kb, j)),
            pl.BlockSpec((tk, tc), lambda i, j, kb: (kb, j)),
        ],
        out_specs=[
            pl.BlockSpec((1, tk, tc), lambda i, j, kb: (i, kb, j)),
            pl.BlockSpec((1, tk, tc), lambda i, j, kb: (i, kb, j)),
            pl.BlockSpec((1, 1, tc), lambda i, j, kb: (i, 0, j)),
        ],
        out_shape=[jax.ShapeDtypeStruct((b, l, c), BF16), jax.ShapeDtypeStruct((b, l, c), BF16),
                   jax.ShapeDtypeStruct((b, 1, c), F32)],
        compiler_params=_cp(("parallel", "parallel", "arbitrary")),
        name="hyena_dft_fwd",
    )(cmat, smat, u, tre, tim)
    tt = tk
    return pl.pallas_call(
        functools.partial(_dft_inv_kernel, l=l, tt=tt),
        grid=(b, c // tc, l // tt),
        in_specs=[
            pl.BlockSpec((tt, l), lambda i, j, tb: (tb, 0)),
            pl.BlockSpec((tt, l), lambda i, j, tb: (tb, 0)),
            pl.BlockSpec((1, l, tc), lambda i, j, tb: (i, 0, j)),
            pl.BlockSpec((1, l, tc), lambda i, j, tb: (i, 0, j)),
            pl.BlockSpec((1, 1, tc), lambda i, j, tb: (i, 0, j)),
            pl.BlockSpec((1, tc), lambda i, j, tb: (0, j)),
            pl.BlockSpec((1, tt, tc), lambda i, j, tb: (i, tb, j)),
            pl.BlockSpec((1, tt, tc), lambda i, j, tb: (i, tb, j)),
            pl.BlockSpec((1, tc), lambda i, j, tb: (0, j)),
        ],
        out_specs=pl.BlockSpec((1, tt, tc), lambda i, j, tb: (i, tb, j)),
        out_shape=jax.ShapeDtypeStruct((b, l, c), BF16),
        compiler_params=_cp(("parallel", "parallel", "arbitrary")),
        name="hyena_dft_inv",
    )(cmat, smat, z1, z2, ny, tny, u, x0, hy_bias.reshape(1, c).astype(F32))


def _cmul(ar, ai, br, bi):
    return ar * br - ai * bi, ar * bi + ai * br


def _s5_prep_kernel(lre_ref, lim_ref, ls_ref, bre_ref, bim_ref, cre_ref, cim_ref,
                    pwr_ref, pwi_ref, wr_ref, wi_ref, clr_ref, cli_ref, k_ref):
    cg = bre_ref.shape[1]
    lr = jnp.minimum(lre_ref[...], -1e-4)
    li = lim_ref[...]
    step = jnp.exp(ls_ref[...])
    mag = jnp.exp(lr * step)
    ar = mag * jnp.cos(li * step)
    ai = mag * jnp.sin(li * step)
    den = lr * lr + li * li
    nr = ar - 1.0
    qr = (nr * lr + ai * li) / den
    qi = (ai * lr - nr * li) / den
    bbr, bbi = _cmul(qr, qi, bre_ref[...], bim_ref[...])
    cr = cre_ref[...]
    ci = cim_ref[...]
    pr = jnp.ones_like(ar)
    pi = jnp.zeros_like(ar)
    for tau in range(S5_T):
        sl = slice(tau * cg, (tau + 1) * cg)
        wr, wi = _cmul(pr, pi, bbr, bbi)
        wr_ref[:, sl, :] = wr
        wi_ref[:, sl, :] = wi
        pr, pi = _cmul(pr, pi, ar, ai)
        clr, cli = _cmul(cr, ci, pr, pi)
        clr_ref[:, sl, :] = clr
        cli_ref[:, sl, :] = cli
    qr, qi = pr, pi
    for m in range(8):
        pwr_ref[:, m:m + 1, :] = qr
        pwi_ref[:, m:m + 1, :] = qi
        qr, qi = _cmul(qr, qi, pr, pi)
    dims = (((2,), (2,)), ((0,), (0,)))
    k_ref[...] = (lax.dot_general(cr, wr_ref[...], dims, precision=HIGHEST, preferred_element_type=F32)
                  - lax.dot_general(ci, wi_ref[...], dims, precision=HIGHEST, preferred_element_type=F32))


def s5_prep(lam_re, lam_im, log_step, b_re, b_im, c_re, c_im):
    _, g, p = lam_re.shape
    cg = b_re.shape[-1]
    n = 2 * g
    tg = _pick(n, (8,))
    flat = lambda a: a.reshape(n, 1, p).astype(F32)
    bt = lambda a: a.reshape(n, p, cg).transpose(0, 2, 1).astype(F32)
    ct = lambda a: a.reshape(n, cg, p).astype(F32)
    s1 = pl.BlockSpec((tg, 1, p), lambda i: (i, 0, 0))
    s3 = pl.BlockSpec((tg, cg, p), lambda i: (i, 0, 0))
    s8 = pl.BlockSpec((tg, 8, p), lambda i: (i, 0, 0))
    st = pl.BlockSpec((tg, S5_T * cg, p), lambda i: (i, 0, 0))
    sk = pl.BlockSpec((tg, cg, S5_T * cg), lambda i: (i, 0, 0))
    big = jax.ShapeDtypeStruct((n, S5_T * cg, p), F32)
    return pl.pallas_call(
        _s5_prep_kernel,
        grid=(n // tg,),
        in_specs=[s1, s1, pl.BlockSpec((tg, 1, 1), lambda i: (i, 0, 0)), s3, s3, s3, s3],
        out_specs=[s8, s8, st, st, st, st, sk],
        out_shape=[jax.ShapeDtypeStruct((n, 8, p), F32)] * 2 + [big] * 4
                  + [jax.ShapeDtypeStruct((n, cg, S5_T * cg), F32)],
        compiler_params=_cp(("parallel",)),
        name="s5_discretise",
    )(flat(lam_re), flat(lam_im), log_step.reshape(n, 1, 1).astype(F32), bt(b_re), bt(b_im), ct(c_re), ct(c_im))


def _gelu(x):
    return 0.5 * x * (1.0 + jnp.tanh(math.sqrt(2.0 / math.pi) * (x + 0.044715 * (x * x * x))))


def _s5_kernel(u_ref, k_ref, b_ref, c_ref, tab_ref, o_ref, x_ref, *, s_sub, gt):
    nblk = u_ref.shape[1] // 8
    ns = x_ref.shape[-1] // 2
    half = ns // 2
    nsteps = (8 // s_sub).bit_length() - 1
    for g in range(gt):
        x_ref[g] = jnp.dot(u_ref[g], b_ref[g], preferred_element_type=F32)
    row = lax.broadcasted_iota(jnp.int32, (8, ns), 0)

    def scan_block(g, vr, vi, cr, ci, reverse):
        for k in range(nsteps):
            sh = s_sub << k
            shift = 8 - sh if reverse else sh
            ar, ai = tab_ref[g, 1 + k, :, :ns], tab_ref[g, 1 + k, :, ns:]
            dr, di = _cmul(ar, ai, pltpu.roll(vr, shift, 0), pltpu.roll(vi, shift, 0))
            vr, vi = vr + dr, vi + di
        if s_sub == 8:
            tr, ti = cr, ci
        else:
            src = (row < s_sub) if reverse else (row >= 8 - s_sub)
            tr = jnp.where(src, cr, 0.0)
            ti = jnp.where(src, ci, 0.0)
            for k in range(nsteps):
                tr = tr + pltpu.roll(tr, s_sub << k, 0)
                ti = ti + pltpu.roll(ti, s_sub << k, 0)
        dr, di = _cmul(tab_ref[g, 0, :, :ns], tab_ref[g, 0, :, ns:], tr, ti)
        fr, fi = vr + dr, vi + di
        if s_sub == 8:
            return tr, ti, fr, fi
        edge = (row >= 8 - s_sub) if reverse else (row < s_sub)
        shift = 8 - s_sub if reverse else s_sub
        xr = jnp.where(edge, tr, pltpu.roll(fr, shift, 0))
        xi = jnp.where(edge, ti, pltpu.roll(fi, shift, 0))
        return xr, xi, fr, fi

    def body(i, carry):
        of = pl.multiple_of(i * 8, 8)
        ob = pl.multiple_of((nblk - 1 - i) * 8, 8)
        new = []
        for g in range(gt):
            cfr, cfi, cbr, cbi = carry[4 * g:4 * g + 4]
            xr, xi, cfr, cfi = scan_block(g, x_ref[g, pl.ds(of, 8), :ns], x_ref[g, pl.ds(of, 8), ns:],
                                          cfr, cfi, False)
            x_ref[g, pl.ds(of, 8), :half] = xr[:, :half]
            x_ref[g, pl.ds(of, 8), ns:ns + half] = xi[:, :half]
            xr, xi, cbr, cbi = scan_block(g, x_ref[g, pl.ds(ob, 8), :ns], x_ref[g, pl.ds(ob, 8), ns:],
                                          cbr, cbi, True)
            x_ref[g, pl.ds(ob, 8), half:ns] = xr[:, half:]
            x_ref[g, pl.ds(ob, 8), ns + half:] = xi[:, half:]
            new += [cfr, cfi, cbr, cbi]
        return tuple(new)

    zero = jnp.zeros((8, ns), F32)
    lax.fori_loop(0, nblk, body, (zero,) * (4 * gt))
    for g in range(gt):
        y = jnp.dot(u_ref[g], k_ref[g], preferred_element_type=F32)
        y = y + jnp.dot(x_ref[g].astype(BF16), c_ref[g], preferred_element_type=F32)
        o_ref[g] = _gelu(y).astype(o_ref.dtype)


def s5_weights(lam_re, lam_im, log_step, b_re, b_im, c_re, c_im, d_skip):
    _, g, p = lam_re.shape
    cg = b_re.shape[-1]
    t = S5_T
    pwr, pwi, wr, wi, clr, cli, kk = s5_prep(lam_re, lam_im, log_step, b_re, b_im, c_re, c_im)
    w4 = lambda a: a.reshape(2, g, t, cg, p)
    wr, wi, clr, cli = w4(wr), w4(wi), w4(clr), w4(cli)
    bm = [wr[0, :, ::-1], wr[1], wi[0, :, ::-1], wi[1]]
    bmat = jnp.concatenate([a.reshape(g, t * cg, p) for a in bm], axis=-1).astype(BF16)
    to_rows = lambda a: a.transpose(0, 3, 1, 2).reshape(g, p, t * cg)
    cm = [to_rows(clr[0]), to_rows(clr[1, :, ::-1]), -to_rows(cli[0]), -to_rows(cli[1, :, ::-1])]
    cmat = jnp.concatenate(cm, axis=1).astype(BF16)
    kk = kk.reshape(2, g, cg, t, cg)
    ts = jnp.arange(t)
    lag = ts[None, :] - ts[:, None]
    kf = jnp.where((lag >= 0)[None, None, :, :, None], kk[0][:, :, jnp.clip(lag, 0, t - 1), :], 0.0)
    kb = jnp.where((lag <= 0)[None, None, :, :, None], kk[1][:, :, jnp.clip(-lag, 0, t - 1), :], 0.0)
    skip = (d_skip.astype(F32).reshape(g, cg)[:, :, None, None, None]
            * jnp.eye(t, dtype=F32)[None, None, :, :, None] * jnp.eye(cg, dtype=F32)[None, :, None, None, :])
    kmat = (kf + kb + skip).transpose(0, 2, 4, 3, 1).reshape(g, t * cg, t * cg).astype(BF16)
    return kmat, bmat, cmat, (pwr.reshape(2, g, 8, p), pwi.reshape(2, g, 8, p))


def s5_scan_tables(pw, s_sub):
    pwr, pwi = pw
    nsteps = (8 // s_sub).bit_length() - 1
    rows = jnp.arange(8)
    tabs = []

    def entry(fwd_idx, bwd_idx, fwd_mask, bwd_mask):
        cols = [pwr[0][:, fwd_idx] * fwd_mask[None, :, None], pwr[1][:, bwd_idx] * bwd_mask[None, :, None],
                pwi[0][:, fwd_idx] * fwd_mask[None, :, None], pwi[1][:, bwd_idx] * bwd_mask[None, :, None]]
        return jnp.concatenate(cols, axis=-1)

    ones = jnp.ones((8,), F32)
    tabs.append(entry(rows // s_sub, (7 - rows) // s_sub, ones, ones))
    for k in range(nsteps):
        sh = s_sub << k
        idx = jnp.full((8,), (1 << k) - 1)
        tabs.append(entry(idx, idx, (rows >= sh).astype(F32), (rows < 8 - sh).astype(F32)))
    return jnp.stack(tabs, axis=1)


def _block_transpose(x):
    n = x.shape[1]
    rows = lax.broadcasted_iota(jnp.int32, x.shape, 0)
    lanes = lax.broadcasted_iota(jnp.int32, x.shape, 1)
    for k in range(4):
        s = 1 << k
        rbit = (rows >> k) & 1
        lbit = (lanes >> (4 + k)) & 1
        up = pltpu.roll(pltpu.roll(x, s, 0), n - 16 * s, 1)
        dn = pltpu.roll(pltpu.roll(x, x.shape[0] - s, 0), 16 * s, 1)
        x = jnp.where((rbit == 1) & (lbit == 0), up, jnp.where((rbit == 0) & (lbit == 1), dn, x))
    return x


def _s5_pack_kernel(x_ref, g_ref, o_ref, lo_ref, hi_ref, *, nb):
    d = x_ref.shape[1]
    hw = lo_ref.shape[1]
    w = 2 * hw
    nj = lo_ref.shape[0] // 16
    if nb is not None:
        live = pl.program_id(0) < nb
    x = x_ref[...]
    inv = lax.rsqrt(jnp.mean(x * x, axis=-1, keepdims=True) + EPS)
    for blk in range(d // w):
        sl = slice(blk * w, (blk + 1) * w)
        xn = x_ref[:, sl] * inv * g_ref[:, sl]
        if nb is not None:
            xn = jnp.where(live, xn, 0.0)
        t = _block_transpose(xn)
        lo_ref[...] = t[:, :hw]
        hi_ref[...] = t[:, hw:]
        for gl in range(16):
            o_ref[blk * 16 + gl, :, :hw] = lo_ref[pl.ds(gl, nj, stride=16), :].astype(o_ref.dtype)
            o_ref[blk * 16 + gl, :, hw:] = hi_ref[pl.ds(gl, nj, stride=16), :].astype(o_ref.dtype)


def _s5_unpack_kernel(y_ref, o_ref, lo_ref, hi_ref):
    hw = lo_ref.shape[1]
    w = 2 * hw
    nj = lo_ref.shape[0] // 16
    for blk in range(o_ref.shape[1] // w):
        for gl in range(16):
            yg = y_ref[blk * 16 + gl].astype(F32)
            lo_ref[gl * nj:(gl + 1) * nj, :] = yg[:, :hw]
            hi_ref[gl * nj:(gl + 1) * nj, :] = yg[:, hw:]
        v = jnp.concatenate(
            [jnp.concatenate([lo_ref[pl.ds(j, 16, stride=nj), :], hi_ref[pl.ds(j, 16, stride=nj), :]], axis=1)
             for j in range(nj)], axis=0)
        o_ref[:, blk * w:(blk + 1) * w] = _block_transpose(v).astype(o_ref.dtype)


def s5_mixer_core(x, gain, b, l, s5w, gt=4):
    d = x.shape[1]
    kmat, bmat, cmat, pw = s5w
    g = kmat.shape[0]
    cg = d // g
    t = S5_T
    nc = l // t
    nj = 16
    w = t * cg
    s_sub = 1 << (b - 1).bit_length()
    assert s_sub <= 8 and (nc * s_sub) % 8 == 0 and g % gt == 0 and cg == 16 and t == 16 and nc % nj == 0
    tab = s5_scan_tables(pw, s_sub)
    tok = nj * t
    lb = l // tok
    u = pl.pallas_call(
        functools.partial(_s5_pack_kernel, nb=None if s_sub == b else b),
        grid=(s_sub, lb),
        in_specs=[pl.BlockSpec((tok, d), lambda i, j: (jnp.minimum(i, b - 1) * lb + j, 0)),
                  pl.BlockSpec((1, d), lambda i, j: (0, 0))],
        out_specs=pl.BlockSpec((g, nj, w), lambda i, j: (0, j, i)),
        out_shape=jax.ShapeDtypeStruct((g, nc, s_sub * w), BF16),
        scratch_shapes=[pltpu.VMEM((nj * 16, w // 2), F32)] * 2,
        compiler_params=_cp(("parallel", "parallel")),
        name="s5_pack",
    )(x, gain.reshape(1, d).astype(F32))
    u = u.reshape(g, nc * s_sub, w)
    r = nc * s_sub
    ns4 = bmat.shape[-1]
    gspec = lambda shape: pl.BlockSpec((gt,) + shape, lambda i: (i,) + (0,) * len(shape))
    y = pl.pallas_call(
        functools.partial(_s5_kernel, s_sub=s_sub, gt=gt),
        grid=(g // gt,),
        in_specs=[gspec((r, w)), gspec((w, w)), gspec((w, ns4)), gspec((ns4, w)), gspec((tab.shape[1], 8, ns4))],
        out_specs=gspec((r, w)),
        out_shape=jax.ShapeDtypeStruct((g, r, w), BF16),
        scratch_shapes=[pltpu.VMEM((gt, r, ns4), F32)],
        compiler_params=_cp(("parallel",)),
        name="s5_chunked",
    )(u, kmat, bmat, cmat, tab)
    return pl.pallas_call(
        _s5_unpack_kernel,
        grid=(b, lb),
        in_specs=[pl.BlockSpec((g, nj, w), lambda i, j: (0, j, i))],
        out_specs=pl.BlockSpec((tok, d), lambda i, j: (i * lb + j, 0)),
        out_shape=jax.ShapeDtypeStruct((b * l, d), BF16),
        scratch_shapes=[pltpu.VMEM((nj * 16, w // 2), F32)] * 2,
        compiler_params=_cp(("parallel", "parallel")),
        name="s5_unpack",
    )(y.reshape(g, nc, s_sub * w))


def _xattn_kernel(q_ref, k_ref, v_ref, qg_ref, kg_ref, o_ref, *, heads):
    scale = HEAD_DIM ** -0.5

    def hnorm(x, g_ref):
        ms = jnp.mean(x * x, axis=-1, keepdims=True)
        return (x * lax.rsqrt(ms + EPS) * g_ref[...]).astype(BF16)

    for h in range(heads):
        sl = slice(h * HEAD_DIM, (h + 1) * HEAD_DIM)
        q = hnorm(q_ref[0, :, sl], qg_ref)
        k = hnorm(k_ref[0, :, sl], kg_ref)
        s = lax.dot_general(q, k, (((1,), (1,)), ((), ())), preferred_element_type=F32) * scale
        m = jnp.max(s, axis=-1, keepdims=True)
        p = jnp.exp(s - m)
        den = jnp.sum(p, axis=-1, keepdims=True)
        o = jnp.dot(p.astype(BF16), v_ref[0, :, sl].astype(BF16), preferred_element_type=F32)
        o_ref[0, :, sl] = (o * (1.0 / den)).astype(o_ref.dtype)


def cross_attention_core(q, k, v, q_gain, k_gain):
    b, l, xw = q.shape
    mm = k.shape[1]
    tq = _pick(l, (512, 256, 128))
    g = lambda a: a.reshape(1, HEAD_DIM).astype(F32)
    return pl.pallas_call(
        functools.partial(_xattn_kernel, heads=xw // HEAD_DIM),
        grid=(b, l // tq),
        in_specs=[
            pl.BlockSpec((1, tq, xw), lambda i, j: (i, j, 0)),
            pl.BlockSpec((1, mm, xw), lambda i, j: (i, 0, 0)),
            pl.BlockSpec((1, mm, xw), lambda i, j: (i, 0, 0)),
            pl.BlockSpec((1, HEAD_DIM), lambda i, j: (0, 0)),
            pl.BlockSpec((1, HEAD_DIM), lambda i, j: (0, 0)),
        ],
        out_specs=pl.BlockSpec((1, tq, xw), lambda i, j: (i, j, 0)),
        out_shape=jax.ShapeDtypeStruct((b, l, xw), BF16),
        compiler_params=_cp(("parallel", "parallel")),
        name="cross_attention",
    )(q, k, v, g(q_gain), g(k_gain))


def kernel(x_prompt, x_sample, mem_prompt, mem_sample, g_mix, g_cross, g_mem, g_ffn, w_in, na_q_gain, na_k_gain, na_rpb, hy_conv_w, hy_conv_b, hy_f_w1, hy_f_b1, hy_f_w2, hy_f_b2, hy_f_w3, hy_f_b3, hy_f_freq, hy_log_decay, hy_bias, w_out, s5_lam_re, s5_lam_im, s5_log_step, s5_b_re, s5_b_im, s5_c_re, s5_c_im, s5_d, w_glu, x_wq, x_wk, x_wv, x_wo, x_q_gain, x_k_gain, w_ffn_gate, w_ffn_up, w_ffn_down):
    depth, d = g_mix.shape
    heads = na_rpb.shape[1]
    naw = heads * HEAD_DIM
    hyw = hy_bias.shape[-1]
    bf = lambda w: w.astype(BF16)

    layers = []
    for layer in range(depth):
        i = layer // 2
        p = {}
        if layer % 2 == 0:
            p["w_in"] = cast_weight(w_in, i)
            p["w_out"] = cast_weight(w_out, i)
            p["table"] = na_table(na_rpb[i])
            p["filters"] = {}
        else:
            p["s5w"] = s5_weights(s5_lam_re[i], s5_lam_im[i], s5_log_step[i], s5_b_re[i], s5_b_im[i],
                                  s5_c_re[i], s5_c_im[i], s5_d[i])
            p["w_glu"] = cast_weight(w_glu, i)
        p["wq"], p["wk"] = cast_weight(x_wq, layer), cast_weight(x_wk, layer)
        p["wv"], p["wo"] = cast_weight(x_wv, layer), cast_weight(x_wo, layer)
        p["ffn"] = (cast_weight(w_ffn_gate, layer), cast_weight(w_ffn_up, layer), cast_weight(w_ffn_down, layer))
        layers.append(p)
    tables = {}

    def filters_for(layer, l):
        p = layers[layer]
        if l not in p["filters"]:
            i = layer // 2
            if l not in tables:
                tables[l] = dft_tables(l)
            cmat, smat = tables[l]
            f1, f2, ny = hyena_filters(l, hy_f_w1[i], hy_f_b1[i], hy_f_w2[i], hy_f_b2[i], hy_f_w3[i],
                                       hy_f_b3[i], hy_f_freq[i], hy_log_decay[i].reshape(-1))
            tre = matmul([cmat], [(f1, 0)], f1.shape[1])
            tim = matmul([smat], [(f2, 0)], f2.shape[1])
            p["filters"][l] = (cmat, smat, tre, tim, ny)
        return p["filters"][l]

    def run(x, mem):
        b, l, _ = x.shape
        nm = mem.shape[1]
        x = x.reshape(b * l, d)
        mem2 = mem.reshape(b * nm, d)
        for layer in range(depth):
            i = layer // 2
            p = layers[layer]
            if layer % 2 == 0:
                proj = matmul([x], [(p["w_in"], 0)], p["w_in"].shape[1], out_dtype=BF16, norm_gain=g_mix[layer])
                proj3 = proj.reshape(b, l, -1)
                y_a = na_attention(proj3, na_q_gain[i], na_k_gain[i], p["table"], heads)
                cmat, smat, tre, tim, tny = filters_for(layer, l)
                u, x0 = hyena_conv_gate(proj3, hy_conv_w[i], hy_conv_b[i], 3 * naw, hyw)
                y_b = hyena_long_conv(u, x0, cmat, smat, tre, tim, tny, hy_bias[i])
                y_a, y_b = y_a.reshape(b * l, naw), y_b.reshape(b * l, hyw)
                ymix = [y_a, y_b] if naw == hyw else [jnp.concatenate([y_a, y_b], axis=-1)]
                x = matmul(ymix, [(p["w_out"], 0)], d, epi="res", res=x)
            else:
                y = s5_mixer_core(x, g_mix[layer], b, l, p["s5w"])
                x = matmul([y], [(p["w_glu"], 0), (p["w_glu"], d)], d, epi="glu_res", res=x)
            mn = rmsnorm(mem2, g_mem[layer], BF16)
            xw = p["wq"].shape[1]
            q = matmul([x], [(p["wq"], 0)], xw, norm_gain=g_cross[layer])
            k = matmul([mn], [(p["wk"], 0)], xw)
            v = matmul([mn], [(p["wv"], 0)], xw)
            o = cross_attention_core(q.reshape(b, l, xw), k.reshape(b, nm, xw), v.reshape(b, nm, xw),
                                     x_q_gain[layer], x_k_gain[layer])
            x = matmul([o.reshape(b * l, xw)], [(p["wo"], 0)], d, epi="res", res=x)
            wg, wu, wd = p["ffn"]
            hdn = matmul([x], [(wg, 0), (wu, 0)], wg.shape[1], epi="swiglu", out_dtype=BF16,
                         norm_gain=g_ffn[layer])
            x = matmul([hdn], [(wd, 0)], d, epi="res", res=x)
        return x.reshape(b, l, d)

    return run(x_prompt, mem_prompt), run(x_sample, mem_sample)
```

```python
import functools
import math

import jax
import jax.numpy as jnp
from jax import lax
from jax.experimental import pallas as pl
from jax.experimental.pallas import tpu as pltpu

F32 = jnp.float32
BF16 = jnp.bfloat16
EPS = 1e-6
HEAD_DIM = 128
GRID_W = 64
NA_ROWS = 8
NA_COLS = 16
NA_RBLK = 8
NA_WIN = 16
HY_BANDS = 16
S5_GROUP = 16
S5_T = 16
NEG = -1e30
HIGHEST = lax.Precision.HIGHEST
VMEM_LIMIT = 56 * 1024 * 1024


def _cp(sem, vmem=VMEM_LIMIT):
    return pltpu.CompilerParams(dimension_semantics=sem, vmem_limit_bytes=vmem)


def _pick(n, cands):
    for c in cands:
        if c <= n and n % c == 0:
            return c
    return n


def _sigmoid(x):
    return 1.0 / (1.0 + jnp.exp(-x))


def _rmsnorm_kernel(x_ref, g_ref, o_ref):
    x = x_ref[...].astype(F32)
    ms = jnp.mean(x * x, axis=-1, keepdims=True)
    o_ref[...] = (x * lax.rsqrt(ms + EPS) * g_ref[...]).astype(o_ref.dtype)


def rmsnorm(x, g, out_dtype):
    m, d = x.shape
    tm = _pick(m, (256, 128, 64, 32, 16, 8))
    return pl.pallas_call(
        _rmsnorm_kernel,
        grid=(m // tm,),
        in_specs=[pl.BlockSpec((tm, d), lambda i: (i, 0)), pl.BlockSpec((1, d), lambda i: (0, 0))],
        out_specs=pl.BlockSpec((tm, d), lambda i: (i, 0)),
        out_shape=jax.ShapeDtypeStruct((m, d), out_dtype),
        compiler_params=_cp(("parallel",)),
        name="rmsnorm",
    )(x, g.reshape(1, d).astype(F32))


def _mm_kernel(*refs, na, nb, epi, norm):
    a_refs = refs[:na]
    b_refs = refs[na:na + na * nb]
    pos = na + na * nb
    res_ref = None
    if epi in ("res", "glu_res"):
        res_ref = refs[pos]
        pos += 1
    if norm:
        g_ref, o_ref, xn_ref = refs[pos], refs[pos + 1], refs[pos + 2]

        @pl.when(pl.program_id(1) == 0)
        def _():
            rows = xn_ref.shape[0]
            ch = _pick(rows, (256,))
            for s in range(0, rows, ch):
                x = a_refs[0][s:s + ch, :]
                inv = lax.rsqrt(jnp.mean(x * x, axis=-1, keepdims=True) + EPS)
                xn_ref[s:s + ch, :] = (x * inv * g_ref[...]).astype(xn_ref.dtype)

        avals = [xn_ref[...]]
    else:
        o_ref = refs[pos]
        avals = [a[...] for a in a_refs]
    parts = []
    for q in range(nb):
        acc = None
        for p, a in enumerate(avals):
            d = jnp.dot(a, b_refs[q * na + p][...], preferred_element_type=F32)
            acc = d if acc is None else acc + d
        parts.append(acc)
    if epi == "plain":
        out = parts[0]
    elif epi == "res":
        out = res_ref[...] + parts[0]
    elif epi == "swiglu":
        out = parts[0] * _sigmoid(parts[0]) * parts[1]
    else:
        out = res_ref[...] + parts[0] * _sigmoid(parts[1])
    o_ref[...] = out.astype(o_ref.dtype)


def matmul(a_parts, bs, n, *, epi="plain", res=None, out_dtype=F32, norm_gain=None):
    m, kp = a_parts[0].shape
    na = len(a_parts)
    k = kp * na
    wide = k <= 4096
    norm = norm_gain is not None
    tm = _pick(m, ((1024,) if wide else ()) + (512, 256, 128, 64, 32, 16, 8))
    tn = _pick(n, (512, 256, 128))
    if norm:
        tm = _pick(m, (512, 256, 128, 64, 32, 16, 8))
    in_specs = [pl.BlockSpec((tm, kp), lambda i, j: (i, 0)) for _ in a_parts]
    args = list(a_parts)
    for b, off in bs:
        assert off % tn == 0 and b.shape[0] == k
        for p in range(na):
            in_specs.append(pl.BlockSpec((kp, tn), functools.partial(lambda i, j, p, ob: (p, ob + j), p=p, ob=off // tn)))
            args.append(b)
    if res is not None:
        in_specs.append(pl.BlockSpec((tm, tn), lambda i, j: (i, j)))
        args.append(res)
    if norm:
        assert na == 1
        in_specs.append(pl.BlockSpec((1, k), lambda i, j: (0, 0)))
        args.append(norm_gain.reshape(1, k).astype(F32))
    return pl.pallas_call(
        functools.partial(_mm_kernel, na=na, nb=len(bs), epi=epi, norm=norm),
        grid=(m // tm, n // tn),
        in_specs=in_specs,
        out_specs=pl.BlockSpec((tm, tn), lambda i, j: (i, j)),
        out_shape=jax.ShapeDtypeStruct((m, n), out_dtype),
        scratch_shapes=[pltpu.VMEM((tm, k), BF16)] if norm else [],
        compiler_params=_cp(("parallel", "arbitrary" if norm else "parallel")),
        name=("mm_norm_" if norm else "mm_") + epi,
    )(*args)


def _cast_kernel(w_ref, o_ref):
    o_ref[...] = w_ref[0].astype(o_ref.dtype)


def cast_weight(w, layer):
    _, k, n = w.shape
    tk = _pick(k, (128, 64, 32, 16))
    return pl.pallas_call(
        _cast_kernel,
        grid=(k // tk,),
        in_specs=[pl.BlockSpec((1, tk, n), lambda i: (layer, i, 0))],
        out_specs=pl.BlockSpec((tk, n), lambda i: (i, 0)),
        out_shape=jax.ShapeDtypeStruct((k, n), BF16),
        compiler_params=_cp(("parallel",)),
        name="cast_weight",
    )(w)


def _na_table_kernel(rpb_ref, o_ref, *, n_ri, n_ci):
    h = pl.program_id(0)
    w = GRID_W
    qc = lax.broadcasted_iota(jnp.int32, (w, w), 0)
    kc = lax.broadcasted_iota(jnp.int32, (w, w), 1)
    ci = kc - qc + (NA_COLS - 1)
    qstart = jnp.clip(qc - NA_COLS // 2, 0, w - NA_COLS)
    valid = (kc >= qstart) & (kc < qstart + NA_COLS)
    neg = jnp.full((w, w), NEG, F32)
    tiles = []
    for ri in range(n_ri):
        t = jnp.zeros((w, w), F32)
        for j in range(n_ci):
            t = jnp.where(ci == j, rpb_ref[h * (n_ri * n_ci) + ri * n_ci + j], t)
        tiles.append(jnp.where(valid, t, neg))
    for e in range(NA_WIN):
        j0 = min(max(e - NA_ROWS // 2, 0), NA_WIN - NA_ROWS)
        for jp in range(NA_WIN // 2):
            pair = []
            for j in (2 * jp, 2 * jp + 1):
                pair.append(tiles[j - e + NA_ROWS - 1] if j0 <= j < j0 + NA_ROWS else neg)
            o_ref[0, e, :, jp * 2 * w:(jp + 1) * 2 * w] = jnp.concatenate(pair, axis=1)


def na_table(rpb):
    h, n_ri, n_ci = rpb.shape
    return pl.pallas_call(
        functools.partial(_na_table_kernel, n_ri=n_ri, n_ci=n_ci),
        grid=(h,),
        in_specs=[pl.BlockSpec(memory_space=pltpu.SMEM)],
        out_specs=pl.BlockSpec((1, NA_WIN, GRID_W, NA_WIN * GRID_W), lambda i: (i, 0, 0, 0)),
        out_shape=jax.ShapeDtypeStruct((h, NA_WIN, GRID_W, NA_WIN * GRID_W), F32),
        compiler_params=_cp(("parallel",)),
        name="na_table",
    )(rpb.reshape(-1).astype(F32))


def _na_kernel(q_ref, k_ref, v_ref, qg_ref, kg_ref, tab_ref, o_ref, qn_ref, kn_ref, *, rows):
    l = rows * GRID_W
    ch = _pick(l, (512,))

    def norm(src, g_ref, dst):
        for c in range(l // ch):
            x = src[0, c * ch:(c + 1) * ch, :].astype(F32)
            ms = jnp.mean(x * x, axis=-1, keepdims=True)
            dst[c * ch:(c + 1) * ch, :] = (x * lax.rsqrt(ms + EPS) * g_ref[...]).astype(BF16)

    norm(q_ref, qg_ref, qn_ref)
    norm(k_ref, kg_ref, kn_ref)
    scale = HEAD_DIM ** -0.5
    nq = NA_RBLK * GRID_W
    nkeys = NA_WIN * GRID_W

    def body(i, carry):
        rb = i * NA_RBLK
        kw = jnp.clip(rb - NA_ROWS // 2, 0, rows - NA_WIN)
        e0 = rb - kw
        q0 = pl.multiple_of(rb * GRID_W, nq)
        k0 = pl.multiple_of(kw * GRID_W, 256)
        q = qn_ref[pl.ds(q0, nq), :]
        kk = kn_ref[pl.ds(k0, nkeys), :]
        vv = v_ref[0, pl.ds(k0, nkeys), :]
        s = lax.dot_general(q, kk, (((1,), (1,)), ((), ())), preferred_element_type=F32) * scale
        s = s.reshape(NA_RBLK, GRID_W, nkeys) + tab_ref[0, pl.ds(e0, NA_RBLK)]
        m = jnp.max(s, axis=-1, keepdims=True)
        p = jnp.exp(s - m)
        den = jnp.sum(p, axis=-1, keepdims=True)
        o = jnp.dot(p.reshape(nq, nkeys).astype(BF16), vv, preferred_element_type=F32)
        o = o * (1.0 / den).reshape(nq, 1)
        o_ref[0, pl.ds(q0, nq), :] = o.astype(o_ref.dtype)
        return carry

    lax.fori_loop(0, rows // NA_RBLK, body, 0)


def na_attention(proj3, q_gain, k_gain, table, heads):
    b, l, _ = proj3.shape
    rows = l // GRID_W
    assert rows % NA_RBLK == 0 and rows >= NA_WIN
    blk = (1, l, HEAD_DIM)
    return pl.pallas_call(
        functools.partial(_na_kernel, rows=rows),
        grid=(b, heads),
        in_specs=[
            pl.BlockSpec(blk, lambda i, h: (i, 0, h)),
            pl.BlockSpec(blk, lambda i, h: (i, 0, heads + h)),
            pl.BlockSpec(blk, lambda i, h: (i, 0, 2 * heads + h)),
            pl.BlockSpec((1, HEAD_DIM), lambda i, h: (0, 0)),
            pl.BlockSpec((1, HEAD_DIM), lambda i, h: (0, 0)),
            pl.BlockSpec((1, NA_WIN, GRID_W, NA_WIN * GRID_W), lambda i, h: (h, 0, 0, 0)),
        ],
        out_specs=pl.BlockSpec(blk, lambda i, h: (i, 0, h)),
        out_shape=jax.ShapeDtypeStruct((b, l, heads * HEAD_DIM), BF16),
        scratch_shapes=[pltpu.VMEM((l, HEAD_DIM), BF16), pltpu.VMEM((l, HEAD_DIM), BF16)],
        compiler_params=_cp(("parallel", "parallel")),
        name="na_attention",
    )(proj3, proj3, proj3, q_gain.reshape(1, HEAD_DIM).astype(F32),
      k_gain.reshape(1, HEAD_DIM).astype(F32), table)


def _filt_kernel(w1t_ref, w1c_ref, w1s_ref, b1_ref, w2_ref, b2_ref, w3_ref, b3_ref, freq_ref, ld_ref,
                 f_ref, sum_ref, *, l, tl):
    i = pl.program_id(0)
    t = (i * tl + lax.broadcasted_iota(jnp.int32, (tl, 1), 0)).astype(F32) / l
    bands = (lax.broadcasted_iota(jnp.int32, (1, HY_BANDS), 1) + 1).astype(F32)
    ang = 2.0 * math.pi * t * bands
    dot = functools.partial(jnp.dot, precision=HIGHEST, preferred_element_type=F32)
    pre = t * w1t_ref[...] + dot(jnp.cos(ang), w1c_ref[...]) + dot(jnp.sin(ang), w1s_ref[...]) + b1_ref[...]
    freq = freq_ref[...]
    hid = jnp.sin(freq * pre)
    hid = jnp.sin(freq * (dot(hid, w2_ref[...]) + b2_ref[...]))
    f = dot(hid, w3_ref[...]) + b3_ref[...]
    f = f * jnp.exp(-jnp.exp(ld_ref[...]) * t)
    f_ref[...] = f
    part = jnp.sum(jnp.abs(f), axis=0, keepdims=True)

    @pl.when(i == 0)
    def _():
        sum_ref[...] = part

    @pl.when(i > 0)
    def _():
        sum_ref[...] += part


def _filt_combine_kernel(f_ref, sum_ref, f1_ref, f2_ref, ny_ref, *, c, tl):
    i = pl.program_id(0)
    tot = sum_ref[:, :c] + sum_ref[:, c:]
    inv = 1.0 / tot
    n = i * tl + lax.broadcasted_iota(jnp.int32, (tl, 1), 0)
    hf = f_ref[:, :c] * inv
    hb = jnp.where(n == 0, 0.0, f_ref[:, c:] * inv)
    f1 = hf + hb
    f1_ref[...] = f1.astype(f1_ref.dtype)
    f2_ref[...] = (hb - hf).astype(f2_ref.dtype)
    sign = (1 - 2 * (n & 1)).astype(F32)
    part = jnp.sum(f1 * sign, axis=0, keepdims=True)

    @pl.when(i == 0)
    def _():
        ny_ref[...] = part

    @pl.when(i > 0)
    def _():
        ny_ref[...] += part


def hyena_filters(l, w1, b1, w2, b2, w3, b3, freq, log_decay):
    hid = w2.shape[0]
    c2 = w3.shape[1]
    c = c2 // 2
    tl = _pick(l, (256,))
    row = lambda a: a.reshape(1, -1).astype(F32)
    full = lambda shape: pl.BlockSpec(shape, lambda i: (0, 0))
    f, s = pl.pallas_call(
        functools.partial(_filt_kernel, l=l, tl=tl),
        grid=(l // tl,),
        in_specs=[full((1, hid)), full((HY_BANDS, hid)), full((HY_BANDS, hid)), full((1, hid)),
                  full((hid, hid)), full((1, hid)), full((hid, c2)), full((1, c2)), full((1, hid)), full((1, c2))],
        out_specs=[pl.BlockSpec((tl, c2), lambda i: (i, 0)), full((1, c2))],
        out_shape=[jax.ShapeDtypeStruct((l, c2), F32), jax.ShapeDtypeStruct((1, c2), F32)],
        compiler_params=_cp(("arbitrary",)),
        name="hyena_filter_mlp",
    )(w1[0:1].astype(F32), w1[1:1 + HY_BANDS].astype(F32), w1[1 + HY_BANDS:].astype(F32), row(b1),
      w2.astype(F32), row(b2), w3.astype(F32), row(b3), row(freq), row(log_decay))
    return pl.pallas_call(
        functools.partial(_filt_combine_kernel, c=c, tl=tl),
        grid=(l // tl,),
        in_specs=[pl.BlockSpec((tl, c2), lambda i: (i, 0)), full((1, c2))],
        out_specs=[pl.BlockSpec((tl, c), lambda i: (i, 0)), pl.BlockSpec((tl, c), lambda i: (i, 0)), full((1, c))],
        out_shape=[jax.ShapeDtypeStruct((l, c), BF16), jax.ShapeDtypeStruct((l, c), BF16),
                   jax.ShapeDtypeStruct((1, c), F32)],
        compiler_params=_cp(("arbitrary",)),
        name="hyena_filter_combine",
    )(f, s)


def _conv_gate_kernel(z0_ref, z1_ref, z2_ref, w0_ref, w1_ref, w2_ref, b0_ref, b1_ref, b2_ref,
                      u_ref, x0_ref, *, l, ch):
    tc = z0_ref.shape[-1]

    def conv(z_ref, w_ref, b_ref, s):
        z = z_ref[0, s:s + ch, :].astype(F32)
        rid = lax.broadcasted_iota(jnp.int32, (ch, tc), 0)
        prev = z_ref[0, s - 1:s, :].astype(F32) if s > 0 else jnp.zeros((1, tc), F32)
        nxt = z_ref[0, s + ch:s + ch + 1, :].astype(F32) if s + ch < l else jnp.zeros((1, tc), F32)
        up = jnp.where(rid == 0, prev, pltpu.roll(z, 1, 0))
        dn = jnp.where(rid == ch - 1, nxt, pltpu.roll(z, ch - 1, 0))
        return w_ref[0:1, :] * up + w_ref[1:2, :] * z + w_ref[2:3, :] * dn + b_ref[...]

    for s in range(0, l, ch):
        x0 = conv(z0_ref, w0_ref, b0_ref, s)
        x1 = conv(z1_ref, w1_ref, b1_ref, s)
        v = conv(z2_ref, w2_ref, b2_ref, s)
        u_ref[0, s:s + ch, :] = (v * x1).astype(u_ref.dtype)
        x0_ref[0, s:s + ch, :] = x0.astype(x0_ref.dtype)


def hyena_conv_gate(proj3, conv_w, conv_b, base, c):
    b, l, _ = proj3.shape
    tc = _pick(math.gcd(base, c), (256, 128))
    ch = _pick(l, (512,))
    nb0, nbc = base // tc, c // tc
    zspec = lambda part: pl.BlockSpec((1, l, tc), lambda i, j: (i, 0, nb0 + part * nbc + j))
    wspec = lambda part: pl.BlockSpec((3, tc), lambda i, j: (0, part * nbc + j))
    bspec = lambda part: pl.BlockSpec((1, tc), lambda i, j: (0, part * nbc + j))
    ospec = pl.BlockSpec((1, l, tc), lambda i, j: (i, 0, j))
    cw = conv_w.astype(F32)
    cb = conv_b.reshape(1, -1).astype(F32)
    return pl.pallas_call(
        functools.partial(_conv_gate_kernel, l=l, ch=ch),
        grid=(b, nbc),
        in_specs=[zspec(0), zspec(1), zspec(2), wspec(0), wspec(1), wspec(2), bspec(0), bspec(1), bspec(2)],
        out_specs=[ospec, ospec],
        out_shape=[jax.ShapeDtypeStruct((b, l, c), BF16), jax.ShapeDtypeStruct((b, l, c), BF16)],
        compiler_params=_cp(("parallel", "parallel")),
        name="hyena_conv_gate",
    )(proj3, proj3, proj3, cw, cw, cw, cb, cb, cb)


def _dft_table_kernel(cb_ref, sb_ref, ck_ref, sk_ref, c_ref, s_ref):
    cb, sb = cb_ref[...], sb_ref[...]
    ck, sk = ck_ref[0], sk_ref[0]
    c_ref[...] = (ck * cb - sk * sb).astype(c_ref.dtype)
    s_ref[...] = (sk * cb + ck * sb).astype(s_ref.dtype)


def dft_tables(l):
    tk = _pick(l, (256,))

    def angles(k, t):
        return ((k * t) % (2 * l)).astype(F32) * (math.pi / l)

    t = lax.broadcasted_iota(jnp.int32, (tk, l), 1)
    base = angles(lax.broadcasted_iota(jnp.int32, (tk, l), 0), t)
    t0 = lax.broadcasted_iota(jnp.int32, (l // tk, 1, l), 2)
    first = angles(lax.broadcasted_iota(jnp.int32, (l // tk, 1, l), 0) * tk, t0)
    full = pl.BlockSpec((tk, l), lambda i: (0, 0))
    row = pl.BlockSpec((1, 1, l), lambda i: (i, 0, 0))
    out = pl.BlockSpec((tk, l), lambda i: (i, 0))
    return pl.pallas_call(
        _dft_table_kernel,
        grid=(l // tk,),
        in_specs=[full, full, row, row],
        out_specs=[out, out],
        out_shape=[jax.ShapeDtypeStruct((l, l), BF16)] * 2,
        compiler_params=_cp(("parallel",)),
        name="dft_tables",
    )(jnp.cos(base), jnp.sin(base), jnp.cos(first), jnp.sin(first))


def _dft_fwd_kernel(c_ref, s_ref, u_ref, tre_ref, tim_ref, z1_ref, z2_ref, ny_ref, *, l, tk):
    kb = pl.program_id(2)
    u = u_ref[0]
    p = jnp.dot(c_ref[...], u, preferred_element_type=F32)
    q = jnp.dot(s_ref[...], u, preferred_element_type=F32)
    kidx = kb * tk + lax.broadcasted_iota(jnp.int32, (tk, 1), 0)
    sc = jnp.where(kidx == 0, 0.5 / l, 1.0 / l)
    tre = tre_ref[...] * sc
    tim = tim_ref[...] * sc
    z1_ref[0] = (p * tre + q * tim).astype(z1_ref.dtype)
    z2_ref[0] = (q * tre - p * tim).astype(z2_ref.dtype)

    @pl.when(kb == 0)
    def _():
        ch = _pick(l, (512,))
        acc = jnp.zeros((1, u_ref.shape[-1]), F32)
        for s in range(0, l, ch):
            n = s + lax.broadcasted_iota(jnp.int32, (ch, 1), 0)
            sign = (1 - 2 * (n & 1)).astype(F32)
            acc = acc + jnp.sum(u_ref[0, s:s + ch, :].astype(F32) * sign, axis=0, keepdims=True)
        ny_ref[0] = acc


def _dft_inv_kernel(c_ref, s_ref, z1_ref, z2_ref, ny_ref, tny_ref, u_ref, x0_ref, hb_ref, o_ref, *, l, tt):
    tb = pl.program_id(2)
    y = jnp.dot(c_ref[...], z1_ref[0], preferred_element_type=F32)
    y = y + jnp.dot(s_ref[...], z2_ref[0], preferred_element_type=F32)
    tidx = tb * tt + lax.broadcasted_iota(jnp.int32, (tt, 1), 0)
    sign = (1 - 2 * (tidx & 1)).astype(F32)
    y = y + sign * (ny_ref[0] * tny_ref[...] * (0.5 / l))
    y = y + u_ref[0].astype(F32) * hb_ref[...]
    o_ref[0] = (y * x0_ref[0].astype(F32)).astype(o_ref.dtype)


def hyena_long_conv(u, x0, cmat, smat, tre, tim, tny, hy_bias):
    b, l, c = u.shape
    tc = _pick(c, (512, 256, 128))
    tk = _pick(l, (512,))
    z1, z2, ny = pl.pallas_call(
        functools.partial(_dft_fwd_kernel, l=l, tk=tk),
        grid=(b, c // tc, l // tk),
        in_specs=[
            pl.BlockSpec((tk, l), lambda i, j, kb: (kb, 0)),
            pl.BlockSpec((tk, l), lambda i, j, kb: (kb, 0)),
            pl.BlockSpec((1, l, tc), lambda i, j, kb: (i, 0, j)),
            pl.BlockSpec((tk, tc), lambda i, j, kb: (kb, j)),
            pl.BlockSpec((tk, tc), lambda i, j, kb: (kb, j)),
        ],
        out_specs=[
            pl.BlockSpec((1, tk, tc), lambda i, j, kb: (i, kb, j)),
            pl.BlockSpec((1, tk, tc), lambda i, j, kb: (i, kb, j)),
            pl.BlockSpec((1, 1, tc), lambda i, j, kb: (i, 0, j)),
        ],
        out_shape=[jax.ShapeDtypeStruct((b, l, c), BF16), jax.ShapeDtypeStruct((b, l, c), BF16),
                   jax.ShapeDtypeStruct((b, 1, c), F32)],
        compiler_params=_cp(("parallel", "parallel", "arbitrary")),
        name="hyena_dft_fwd",
    )(cmat, smat, u, tre, tim)
    tt = tk
    return pl.pallas_call(
        functools.partial(_dft_inv_kernel, l=l, tt=tt),
        grid=(b, c // tc, l // tt),
        in_specs=[
            pl.BlockSpec((tt, l), lambda i, j, tb: (tb, 0)),
            pl.BlockSpec((tt, l), lambda i, j, tb: (tb, 0)),
            pl.BlockSpec((1, l, tc), lambda i, j, tb: (i, 0, j)),
            pl.BlockSpec((1, l, tc), lambda i, j, tb: (i, 0, j)),
            pl.BlockSpec((1, 1, tc), lambda i, j, tb: (i, 0, j)),
            pl.BlockSpec((1, tc), lambda i, j, tb: (0, j)),
            pl.BlockSpec((1, tt, tc), lambda i, j, tb: (i, tb, j)),
            pl.BlockSpec((1, tt, tc), lambda i, j, tb: (i, tb, j)),
            pl.BlockSpec((1, tc), lambda i, j, tb: (0, j)),
        ],
        out_specs=pl.BlockSpec((1, tt, tc), lambda i, j, tb: (i, tb, j)),
        out_shape=jax.ShapeDtypeStruct((b, l, c), BF16),
        compiler_params=_cp(("parallel", "parallel", "arbitrary")),
        name="hyena_dft_inv",
    )(cmat, smat, z1, z2, ny, tny, u, x0, hy_bias.reshape(1, c).astype(F32))


def _cmul(ar, ai, br, bi):
    return ar * br - ai * bi, ar * bi + ai * br


def _s5_prep_kernel(lre_ref, lim_ref, ls_ref, bre_ref, bim_ref, cre_ref, cim_ref,
                    pwr_ref, pwi_ref, wr_ref, wi_ref, clr_ref, cli_ref, k_ref):
    cg = bre_ref.shape[1]
    lr = jnp.minimum(lre_ref[...], -1e-4)
    li = lim_ref[...]
    step = jnp.exp(ls_ref[...])
    mag = jnp.exp(lr * step)
    ar = mag * jnp.cos(li * step)
    ai = mag * jnp.sin(li * step)
    den = lr * lr + li * li
    nr = ar - 1.0
    qr = (nr * lr + ai * li) / den
    qi = (ai * lr - nr * li) / den
    bbr, bbi = _cmul(qr, qi, bre_ref[...], bim_ref[...])
    cr = cre_ref[...]
    ci = cim_ref[...]
    pr = jnp.ones_like(ar)
    pi = jnp.zeros_like(ar)
    for tau in range(S5_T):
        sl = slice(tau * cg, (tau + 1) * cg)
        wr, wi = _cmul(pr, pi, bbr, bbi)
        wr_ref[:, sl, :] = wr
        wi_ref[:, sl, :] = wi
        pr, pi = _cmul(pr, pi, ar, ai)
        clr, cli = _cmul(cr, ci, pr, pi)
        clr_ref[:, sl, :] = clr
        cli_ref[:, sl, :] = cli
    qr, qi = pr, pi
    for m in range(8):
        pwr_ref[:, m:m + 1, :] = qr
        pwi_ref[:, m:m + 1, :] = qi
        qr, qi = _cmul(qr, qi, pr, pi)
    dims = (((2,), (2,)), ((0,), (0,)))
    k_ref[...] = (lax.dot_general(cr, wr_ref[...], dims, precision=HIGHEST, preferred_element_type=F32)
                  - lax.dot_general(ci, wi_ref[...], dims, precision=HIGHEST, preferred_element_type=F32))


def s5_prep(lam_re, lam_im, log_step, b_re, b_im, c_re, c_im):
    _, g, p = lam_re.shape
    cg = b_re.shape[-1]
    n = 2 * g
    tg = _pick(n, (8,))
    flat = lambda a: a.reshape(n, 1, p).astype(F32)
    bt = lambda a: a.reshape(n, p, cg).transpose(0, 2, 1).astype(F32)
    ct = lambda a: a.reshape(n, cg, p).astype(F32)
    s1 = pl.BlockSpec((tg, 1, p), lambda i: (i, 0, 0))
    s3 = pl.BlockSpec((tg, cg, p), lambda i: (i, 0, 0))
    s8 = pl.BlockSpec((tg, 8, p), lambda i: (i, 0, 0))
    st = pl.BlockSpec((tg, S5_T * cg, p), lambda i: (i, 0, 0))
    sk = pl.BlockSpec((tg, cg, S5_T * cg), lambda i: (i, 0, 0))
    big = jax.ShapeDtypeStruct((n, S5_T * cg, p), F32)
    return pl.pallas_call(
        _s5_prep_kernel,
        grid=(n // tg,),
        in_specs=[s1, s1, pl.BlockSpec((tg, 1, 1), lambda i: (i, 0, 0)), s3, s3, s3, s3],
        out_specs=[s8, s8, st, st, st, st, sk],
        out_shape=[jax.ShapeDtypeStruct((n, 8, p), F32)] * 2 + [big] * 4
                  + [jax.ShapeDtypeStruct((n, cg, S5_T * cg), F32)],
        compiler_params=_cp(("parallel",)),
        name="s5_discretise",
    )(flat(lam_re), flat(lam_im), log_step.reshape(n, 1, 1).astype(F32), bt(b_re), bt(b_im), ct(c_re), ct(c_im))


def _gelu(x):
    return 0.5 * x * (1.0 + jnp.tanh(math.sqrt(2.0 / math.pi) * (x + 0.044715 * (x * x * x))))


def _s5_kernel(u_ref, k_ref, b_ref, c_ref, tab_ref, o_ref, x_ref, *, s_sub, gt):
    nblk = u_ref.shape[1] // 8
    ns = x_ref.shape[-1] // 2
    half = ns // 2
    nsteps = (8 // s_sub).bit_length() - 1
    for g in range(gt):
        x_ref[g] = jnp.dot(u_ref[g], b_ref[g], preferred_element_type=F32)
    row = lax.broadcasted_iota(jnp.int32, (8, ns), 0)

    def scan_block(g, vr, vi, cr, ci, reverse):
        for k in range(nsteps):
            sh = s_sub << k
            shift = 8 - sh if reverse else sh
            ar, ai = tab_ref[g, 1 + k, :, :ns], tab_ref[g, 1 + k, :, ns:]
            dr, di = _cmul(ar, ai, pltpu.roll(vr, shift, 0), pltpu.roll(vi, shift, 0))
            vr, vi = vr + dr, vi + di
        if s_sub == 8:
            tr, ti = cr, ci
        else:
            src = (row < s_sub) if reverse else (row >= 8 - s_sub)
            tr = jnp.where(src, cr, 0.0)
            ti = jnp.where(src, ci, 0.0)
            for k in range(nsteps):
                tr = tr + pltpu.roll(tr, s_sub << k, 0)
                ti = ti + pltpu.roll(ti, s_sub << k, 0)
        dr, di = _cmul(tab_ref[g, 0, :, :ns], tab_ref[g, 0, :, ns:], tr, ti)
        fr, fi = vr + dr, vi + di
        if s_sub == 8:
            return tr, ti, fr, fi
        edge = (row >= 8 - s_sub) if reverse else (row < s_sub)
        shift = 8 - s_sub if reverse else s_sub
        xr = jnp.where(edge, tr, pltpu.roll(fr, shift, 0))
        xi = jnp.where(edge, ti, pltpu.roll(fi, shift, 0))
        return xr, xi, fr, fi

    def body(i, carry):
        of = pl.multiple_of(i * 8, 8)
        ob = pl.multiple_of((nblk - 1 - i) * 8, 8)
        new = []
        for g in range(gt):
            cfr, cfi, cbr, cbi = carry[4 * g:4 * g + 4]
            xr, xi, cfr, cfi = scan_block(g, x_ref[g, pl.ds(of, 8), :ns], x_ref[g, pl.ds(of, 8), ns:],
                                          cfr, cfi, False)
            x_ref[g, pl.ds(of, 8), :half] = xr[:, :half]
            x_ref[g, pl.ds(of, 8), ns:ns + half] = xi[:, :half]
            xr, xi, cbr, cbi = scan_block(g, x_ref[g, pl.ds(ob, 8), :ns], x_ref[g, pl.ds(ob, 8), ns:],
                                          cbr, cbi, True)
            x_ref[g, pl.ds(ob, 8), half:ns] = xr[:, half:]
            x_ref[g, pl.ds(ob, 8), ns + half:] = xi[:, half:]
            new += [cfr, cfi, cbr, cbi]
        return tuple(new)

    zero = jnp.zeros((8, ns), F32)
    lax.fori_loop(0, nblk, body, (zero,) * (4 * gt))
    for g in range(gt):
        y = jnp.dot(u_ref[g], k_ref[g], preferred_element_type=F32)
        y = y + jnp.dot(x_ref[g].astype(BF16), c_ref[g], preferred_element_type=F32)
        o_ref[g] = _gelu(y).astype(o_ref.dtype)


def s5_weights(lam_re, lam_im, log_step, b_re, b_im, c_re, c_im, d_skip):
    _, g, p = lam_re.shape
    cg = b_re.shape[-1]
    t = S5_T
    pwr, pwi, wr, wi, clr, cli, kk = s5_prep(lam_re, lam_im, log_step, b_re, b_im, c_re, c_im)
    w4 = lambda a: a.reshape(2, g, t, cg, p)
    wr, wi, clr, cli = w4(wr), w4(wi), w4(clr), w4(cli)
    bm = [wr[0, :, ::-1], wr[1], wi[0, :, ::-1], wi[1]]
    bmat = jnp.concatenate([a.reshape(g, t * cg, p) for a in bm], axis=-1).astype(BF16)
    to_rows = lambda a: a.transpose(0, 3, 1, 2).reshape(g, p, t * cg)
    cm = [to_rows(clr[0]), to_rows(clr[1, :, ::-1]), -to_rows(cli[0]), -to_rows(cli[1, :, ::-1])]
    cmat = jnp.concatenate(cm, axis=1).astype(BF16)
    kk = kk.reshape(2, g, cg, t, cg)
    ts = jnp.arange(t)
    lag = ts[None, :] - ts[:, None]
    kf = jnp.where((lag >= 0)[None, None, :, :, None], kk[0][:, :, jnp.clip(lag, 0, t - 1), :], 0.0)
    kb = jnp.where((lag <= 0)[None, None, :, :, None], kk[1][:, :, jnp.clip(-lag, 0, t - 1), :], 0.0)
    skip = (d_skip.astype(F32).reshape(g, cg)[:, :, None, None, None]
            * jnp.eye(t, dtype=F32)[None, None, :, :, None] * jnp.eye(cg, dtype=F32)[None, :, None, None, :])
    kmat = (kf + kb + skip).transpose(0, 2, 4, 3, 1).reshape(g, t * cg, t * cg).astype(BF16)
    return kmat, bmat, cmat, (pwr.reshape(2, g, 8, p), pwi.reshape(2, g, 8, p))


def s5_scan_tables(pw, s_sub):
    pwr, pwi = pw
    nsteps = (8 // s_sub).bit_length() - 1
    rows = jnp.arange(8)
    tabs = []

    def entry(fwd_idx, bwd_idx, fwd_mask, bwd_mask):
        cols = [pwr[0][:, fwd_idx] * fwd_mask[None, :, None], pwr[1][:, bwd_idx] * bwd_mask[None, :, None],
                pwi[0][:, fwd_idx] * fwd_mask[None, :, None], pwi[1][:, bwd_idx] * bwd_mask[None, :, None]]
        return jnp.concatenate(cols, axis=-1)

    ones = jnp.ones((8,), F32)
    tabs.append(entry(rows // s_sub, (7 - rows) // s_sub, ones, ones))
    for k in range(nsteps):
        sh = s_sub << k
        idx = jnp.full((8,), (1 << k) - 1)
        tabs.append(entry(idx, idx, (rows >= sh).astype(F32), (rows < 8 - sh).astype(F32)))
    return jnp.stack(tabs, axis=1)


def _pack2(a, b):
    ua = lax.bitcast_convert_type(a.astype(BF16).astype(F32), jnp.int32)
    ub = lax.bitcast_convert_type(b.astype(BF16).astype(F32), jnp.int32)
    return ua | lax.shift_right_logical(ub, 16)


def _unpack2(p):
    a = lax.bitcast_convert_type(p & jnp.int32(-65536), F32)
    b = lax.bitcast_convert_type(lax.shift_left(p, 16), F32)
    return a, b


def _block_transpose(x):
    n = x.shape[1]
    rows = lax.broadcasted_iota(jnp.int32, x.shape, 0)
    lanes = lax.broadcasted_iota(jnp.int32, x.shape, 1)
    for k in range(4):
        s = 1 << k
        rbit = (rows >> k) & 1
        lbit = (lanes >> (4 + k)) & 1
        up = pltpu.roll(pltpu.roll(x, s, 0), n - 16 * s, 1)
        dn = pltpu.roll(pltpu.roll(x, x.shape[0] - s, 0), 16 * s, 1)
        x = jnp.where((rbit == 1) & (lbit == 0), up, jnp.where((rbit == 0) & (lbit == 1), dn, x))
    return x


def _s5_pack_kernel(x_ref, g_ref, o_ref, lo_ref, hi_ref, *, nb):
    d = x_ref.shape[1]
    hw = lo_ref.shape[1]
    w = 2 * hw
    nj = lo_ref.shape[0] // 16
    if nb is not None:
        live = pl.program_id(0) < nb
    x = x_ref[...]
    inv = lax.rsqrt(jnp.mean(x * x, axis=-1, keepdims=True) + EPS)
    for pair in range(d // (2 * w)):
        xs = []
        for blk in (2 * pair, 2 * pair + 1):
            sl = slice(blk * w, (blk + 1) * w)
            xn = x_ref[:, sl] * inv * g_ref[:, sl]
            if nb is not None:
                xn = jnp.where(live, xn, 0.0)
            xs.append(xn)
        t = _block_transpose(_pack2(*xs))
        lo_ref[...] = t[:, :hw]
        hi_ref[...] = t[:, hw:]
        for gl in range(16):
            lo = _unpack2(lo_ref[pl.ds(gl, nj, stride=16), :])
            hi = _unpack2(hi_ref[pl.ds(gl, nj, stride=16), :])
            for q in range(2):
                o_ref[(2 * pair + q) * 16 + gl, :, :hw] = lo[q].astype(o_ref.dtype)
                o_ref[(2 * pair + q) * 16 + gl, :, hw:] = hi[q].astype(o_ref.dtype)


def _s5_unpack_kernel(y_ref, o_ref, lo_ref, hi_ref):
    hw = lo_ref.shape[1]
    w = 2 * hw
    nj = lo_ref.shape[0] // 16
    for pair in range(o_ref.shape[1] // (2 * w)):
        for gl in range(16):
            yg = _pack2(y_ref[2 * pair * 16 + gl].astype(F32), y_ref[(2 * pair + 1) * 16 + gl].astype(F32))
            lo_ref[gl * nj:(gl + 1) * nj, :] = yg[:, :hw]
            hi_ref[gl * nj:(gl + 1) * nj, :] = yg[:, hw:]
        v = jnp.concatenate(
            [jnp.concatenate([lo_ref[pl.ds(j, 16, stride=nj), :], hi_ref[pl.ds(j, 16, stride=nj), :]], axis=1)
             for j in range(nj)], axis=0)
        t = _unpack2(_block_transpose(v))
        for q in range(2):
            o_ref[:, (2 * pair + q) * w:(2 * pair + q + 1) * w] = t[q].astype(o_ref.dtype)


def s5_mixer_core(x, gain, b, l, s5w, gt=4):
    d = x.shape[1]
    kmat, bmat, cmat, pw = s5w
    g = kmat.shape[0]
    cg = d // g
    t = S5_T
    nc = l // t
    nj = 16
    w = t * cg
    s_sub = 1 << (b - 1).bit_length()
    assert s_sub <= 8 and (nc * s_sub) % 8 == 0 and g % gt == 0 and cg == 16 and t == 16 and nc % nj == 0
    tab = s5_scan_tables(pw, s_sub)
    tok = nj * t
    lb = l // tok
    u = pl.pallas_call(
        functools.partial(_s5_pack_kernel, nb=None if s_sub == b else b),
        grid=(s_sub, lb),
        in_specs=[pl.BlockSpec((tok, d), lambda i, j: (jnp.minimum(i, b - 1) * lb + j, 0)),
                  pl.BlockSpec((1, d), lambda i, j: (0, 0))],
        out_specs=pl.BlockSpec((g, nj, w), lambda i, j: (0, j, i)),
        out_shape=jax.ShapeDtypeStruct((g, nc, s_sub * w), BF16),
        scratch_shapes=[pltpu.VMEM((nj * 16, w // 2), jnp.int32)] * 2,
        compiler_params=_cp(("parallel", "parallel")),
        name="s5_pack",
    )(x, gain.reshape(1, d).astype(F32))
    u = u.reshape(g, nc * s_sub, w)
    r = nc * s_sub
    ns4 = bmat.shape[-1]
    gspec = lambda shape: pl.BlockSpec((gt,) + shape, lambda i: (i,) + (0,) * len(shape))
    y = pl.pallas_call(
        functools.partial(_s5_kernel, s_sub=s_sub, gt=gt),
        grid=(g // gt,),
        in_specs=[gspec((r, w)), gspec((w, w)), gspec((w, ns4)), gspec((ns4, w)), gspec((tab.shape[1], 8, ns4))],
        out_specs=gspec((r, w)),
        out_shape=jax.ShapeDtypeStruct((g, r, w), BF16),
        scratch_shapes=[pltpu.VMEM((gt, r, ns4), F32)],
        compiler_params=_cp(("parallel",)),
        name="s5_chunked",
    )(u, kmat, bmat, cmat, tab)
    return pl.pallas_call(
        _s5_unpack_kernel,
        grid=(b, lb),
        in_specs=[pl.BlockSpec((g, nj, w), lambda i, j: (0, j, i))],
        out_specs=pl.BlockSpec((tok, d), lambda i, j: (i * lb + j, 0)),
        out_shape=jax.ShapeDtypeStruct((b * l, d), BF16),
        scratch_shapes=[pltpu.VMEM((nj * 16, w // 2), jnp.int32)] * 2,
        compiler_params=_cp(("parallel", "parallel")),
        name="s5_unpack",
    )(y.reshape(g, nc, s_sub * w))


def _xattn_kernel(q_ref, k_ref, v_ref, qg_ref, kg_ref, o_ref, *, heads):
    scale = HEAD_DIM ** -0.5

    def hnorm(x, g_ref):
        ms = jnp.mean(x * x, axis=-1, keepdims=True)
        return (x * lax.rsqrt(ms + EPS) * g_ref[...]).astype(BF16)

    for h in range(heads):
        sl = slice(h * HEAD_DIM, (h + 1) * HEAD_DIM)
        q = hnorm(q_ref[0, :, sl], qg_ref)
        k = hnorm(k_ref[0, :, sl], kg_ref)
        s = lax.dot_general(q, k, (((1,), (1,)), ((), ())), preferred_element_type=F32) * scale
        m = jnp.max(s, axis=-1, keepdims=True)
        p = jnp.exp(s - m)
        den = jnp.sum(p, axis=-1, keepdims=True)
        o = jnp.dot(p.astype(BF16), v_ref[0, :, sl].astype(BF16), preferred_element_type=F32)
        o_ref[0, :, sl] = (o * (1.0 / den)).astype(o_ref.dtype)


def cross_attention_core(q, k, v, q_gain, k_gain):
    b, l, xw = q.shape
    mm = k.shape[1]
    tq = _pick(l, (512, 256, 128))
    g = lambda a: a.reshape(1, HEAD_DIM).astype(F32)
    return pl.pallas_call(
        functools.partial(_xattn_kernel, heads=xw // HEAD_DIM),
        grid=(b, l // tq),
        in_specs=[
            pl.BlockSpec((1, tq, xw), lambda i, j: (i, j, 0)),
            pl.BlockSpec((1, mm, xw), lambda i, j: (i, 0, 0)),
            pl.BlockSpec((1, mm, xw), lambda i, j: (i, 0, 0)),
            pl.BlockSpec((1, HEAD_DIM), lambda i, j: (0, 0)),
            pl.BlockSpec((1, HEAD_DIM), lambda i, j: (0, 0)),
        ],
        out_specs=pl.BlockSpec((1, tq, xw), lambda i, j: (i, j, 0)),
        out_shape=jax.ShapeDtypeStruct((b, l, xw), BF16),
        compiler_params=_cp(("parallel", "parallel")),
        name="cross_attention",
    )(q, k, v, g(q_gain), g(k_gain))


def kernel(x_prompt, x_sample, mem_prompt, mem_sample, g_mix, g_cross, g_mem, g_ffn, w_in, na_q_gain, na_k_gain, na_rpb, hy_conv_w, hy_conv_b, hy_f_w1, hy_f_b1, hy_f_w2, hy_f_b2, hy_f_w3, hy_f_b3, hy_f_freq, hy_log_decay, hy_bias, w_out, s5_lam_re, s5_lam_im, s5_log_step, s5_b_re, s5_b_im, s5_c_re, s5_c_im, s5_d, w_glu, x_wq, x_wk, x_wv, x_wo, x_q_gain, x_k_gain, w_ffn_gate, w_ffn_up, w_ffn_down):
    depth, d = g_mix.shape
    heads = na_rpb.shape[1]
    naw = heads * HEAD_DIM
    hyw = hy_bias.shape[-1]
    bf = lambda w: w.astype(BF16)

    layers = []
    for layer in range(depth):
        i = layer // 2
        p = {}
        if layer % 2 == 0:
            p["w_in"] = cast_weight(w_in, i)
            p["w_out"] = cast_weight(w_out, i)
            p["table"] = na_table(na_rpb[i])
            p["filters"] = {}
        else:
            p["s5w"] = s5_weights(s5_lam_re[i], s5_lam_im[i], s5_log_step[i], s5_b_re[i], s5_b_im[i],
                                  s5_c_re[i], s5_c_im[i], s5_d[i])
            p["w_glu"] = cast_weight(w_glu, i)
        p["wq"], p["wk"] = cast_weight(x_wq, layer), cast_weight(x_wk, layer)
        p["wv"], p["wo"] = cast_weight(x_wv, layer), cast_weight(x_wo, layer)
        p["ffn"] = (cast_weight(w_ffn_gate, layer), cast_weight(w_ffn_up, layer), cast_weight(w_ffn_down, layer))
        layers.append(p)
    tables = {}

    def filters_for(layer, l):
        p = layers[layer]
        if l not in p["filters"]:
            i = layer // 2
            if l not in tables:
                tables[l] = dft_tables(l)
            cmat, smat = tables[l]
            f1, f2, ny = hyena_filters(l, hy_f_w1[i], hy_f_b1[i], hy_f_w2[i], hy_f_b2[i], hy_f_w3[i],
                                       hy_f_b3[i], hy_f_freq[i], hy_log_decay[i].reshape(-1))
            tre = matmul([cmat], [(f1, 0)], f1.shape[1])
            tim = matmul([smat], [(f2, 0)], f2.shape[1])
            p["filters"][l] = (cmat, smat, tre, tim, ny)
        return p["filters"][l]

    def run(x, mem):
        b, l, _ = x.shape
        nm = mem.shape[1]
        x = x.reshape(b * l, d)
        mem2 = mem.reshape(b * nm, d)
        for layer in range(depth):
            i = layer // 2
            p = layers[layer]
            if layer % 2 == 0:
                xn = rmsnorm(x, g_mix[layer], BF16)
                proj = matmul([xn], [(p["w_in"], 0)], p["w_in"].shape[1], out_dtype=BF16)
                proj3 = proj.reshape(b, l, -1)
                y_a = na_attention(proj3, na_q_gain[i], na_k_gain[i], p["table"], heads)
                cmat, smat, tre, tim, tny = filters_for(layer, l)
                u, x0 = hyena_conv_gate(proj3, hy_conv_w[i], hy_conv_b[i], 3 * naw, hyw)
                y_b = hyena_long_conv(u, x0, cmat, smat, tre, tim, tny, hy_bias[i])
                y_a, y_b = y_a.reshape(b * l, naw), y_b.reshape(b * l, hyw)
                ymix = [y_a, y_b] if naw == hyw else [jnp.concatenate([y_a, y_b], axis=-1)]
                x = matmul(ymix, [(p["w_out"], 0)], d, epi="res", res=x)
            else:
                y = s5_mixer_core(x, g_mix[layer], b, l, p["s5w"])
                x = matmul([y], [(p["w_glu"], 0), (p["w_glu"], d)], d, epi="glu_res", res=x)
            mn = rmsnorm(mem2, g_mem[layer], BF16)
            xw = p["wq"].shape[1]
            q = matmul([x], [(p["wq"], 0)], xw, norm_gain=g_cross[layer])
            k = matmul([mn], [(p["wk"], 0)], xw)
            v = matmul([mn], [(p["wv"], 0)], xw)
            o = cross_attention_core(q.reshape(b, l, xw), k.reshape(b, nm, xw), v.reshape(b, nm, xw),
                                     x_q_gain[layer], x_k_gain[layer])
            x = matmul([o.reshape(b * l, xw)], [(p["wo"], 0)], d, epi="res", res=x)
            wg, wu, wd = p["ffn"]
            xn = rmsnorm(x, g_ffn[layer], BF16)
            hdn = matmul([xn], [(wg, 0), (wu, 0)], wg.shape[1], epi="swiglu", out_dtype=BF16)
            x = matmul([hdn], [(wd, 0)], d, epi="res", res=x)
        return x.reshape(b, l, d)

    return run(x_prompt, mem_prompt), run(x_sample, mem_sample)
```

```python
import functools
import math

import jax
import jax.numpy as jnp
from jax import lax
from jax.experimental import pallas as pl
from jax.experimental.pallas import tpu as pltpu

F32 = jnp.float32
BF16 = jnp.bfloat16
EPS = 1e-6
HEAD_DIM = 128
GRID_W = 64
NA_ROWS = 8
NA_COLS = 16
NA_RBLK = 8
NA_WIN = 16
HY_BANDS = 16
S5_GROUP = 16
S5_T = 16
NEG = -1e30
HIGHEST = lax.Precision.HIGHEST
VMEM_LIMIT = 56 * 1024 * 1024


def _cp(sem, vmem=VMEM_LIMIT):
    return pltpu.CompilerParams(dimension_semantics=sem, vmem_limit_bytes=vmem)


def _pick(n, cands):
    for c in cands:
        if c <= n and n % c == 0:
            return c
    return n


def _sigmoid(x):
    return 1.0 / (1.0 + jnp.exp(-x))


def _rmsnorm_kernel(x_ref, g_ref, o_ref):
    x = x_ref[...].astype(F32)
    ms = jnp.mean(x * x, axis=-1, keepdims=True)
    o_ref[...] = (x * lax.rsqrt(ms + EPS) * g_ref[...]).astype(o_ref.dtype)


def rmsnorm(x, g, out_dtype):
    m, d = x.shape
    tm = _pick(m, (256, 128, 64, 32, 16, 8))
    return pl.pallas_call(
        _rmsnorm_kernel,
        grid=(m // tm,),
        in_specs=[pl.BlockSpec((tm, d), lambda i: (i, 0)), pl.BlockSpec((1, d), lambda i: (0, 0))],
        out_specs=pl.BlockSpec((tm, d), lambda i: (i, 0)),
        out_shape=jax.ShapeDtypeStruct((m, d), out_dtype),
        compiler_params=_cp(("parallel",)),
        name="rmsnorm",
    )(x, g.reshape(1, d).astype(F32))


def _mm_kernel(*refs, na, nb, epi, norm):
    a_refs = refs[:na]
    b_refs = refs[na:na + na * nb]
    pos = na + na * nb
    res_ref = None
    if epi in ("res", "glu_res"):
        res_ref = refs[pos]
        pos += 1
    if norm:
        g_ref, o_ref, xn_ref = refs[pos], refs[pos + 1], refs[pos + 2]

        @pl.when(pl.program_id(1) == 0)
        def _():
            rows = xn_ref.shape[0]
            ch = _pick(rows, (256,))
            for s in range(0, rows, ch):
                x = a_refs[0][s:s + ch, :]
                inv = lax.rsqrt(jnp.mean(x * x, axis=-1, keepdims=True) + EPS)
                xn_ref[s:s + ch, :] = (x * inv * g_ref[...]).astype(xn_ref.dtype)

        avals = [xn_ref[...]]
    else:
        o_ref = refs[pos]
        avals = [a[...] for a in a_refs]
    parts = []
    for q in range(nb):
        acc = None
        for p, a in enumerate(avals):
            d = jnp.dot(a, b_refs[q * na + p][...], preferred_element_type=F32)
            acc = d if acc is None else acc + d
        parts.append(acc)
    if epi == "plain":
        out = parts[0]
    elif epi == "res":
        out = res_ref[...] + parts[0]
    elif epi == "swiglu":
        out = parts[0] * _sigmoid(parts[0]) * parts[1]
    else:
        out = res_ref[...] + parts[0] * _sigmoid(parts[1])
    o_ref[...] = out.astype(o_ref.dtype)


def matmul(a_parts, bs, n, *, epi="plain", res=None, out_dtype=F32, norm_gain=None):
    m, kp = a_parts[0].shape
    na = len(a_parts)
    k = kp * na
    wide = k <= 4096
    norm = norm_gain is not None
    tm = _pick(m, ((1024,) if wide else ()) + (512, 256, 128, 64, 32, 16, 8))
    tn = _pick(n, (512, 256, 128))
    if norm:
        tm = _pick(m, (512, 256, 128, 64, 32, 16, 8))
    in_specs = [pl.BlockSpec((tm, kp), lambda i, j: (i, 0)) for _ in a_parts]
    args = list(a_parts)
    for b, off in bs:
        assert off % tn == 0 and b.shape[0] == k
        for p in range(na):
            in_specs.append(pl.BlockSpec((kp, tn), functools.partial(lambda i, j, p, ob: (p, ob + j), p=p, ob=off // tn)))
            args.append(b)
    if res is not None:
        in_specs.append(pl.BlockSpec((tm, tn), lambda i, j: (i, j)))
        args.append(res)
    if norm:
        assert na == 1
        in_specs.append(pl.BlockSpec((1, k), lambda i, j: (0, 0)))
        args.append(norm_gain.reshape(1, k).astype(F32))
    return pl.pallas_call(
        functools.partial(_mm_kernel, na=na, nb=len(bs), epi=epi, norm=norm),
        grid=(m // tm, n // tn),
        in_specs=in_specs,
        out_specs=pl.BlockSpec((tm, tn), lambda i, j: (i, j)),
        out_shape=jax.ShapeDtypeStruct((m, n), out_dtype),
        scratch_shapes=[pltpu.VMEM((tm, k), BF16)] if norm else [],
        compiler_params=_cp(("parallel", "arbitrary" if norm else "parallel")),
        name=("mm_norm_" if norm else "mm_") + epi,
    )(*args)


def _cast_kernel(w_ref, o_ref):
    o_ref[...] = w_ref[0].astype(o_ref.dtype)


def cast_weight(w, layer):
    _, k, n = w.shape
    tk = _pick(k, (128, 64, 32, 16))
    return pl.pallas_call(
        _cast_kernel,
        grid=(k // tk,),
        in_specs=[pl.BlockSpec((1, tk, n), lambda i: (layer, i, 0))],
        out_specs=pl.BlockSpec((tk, n), lambda i: (i, 0)),
        out_shape=jax.ShapeDtypeStruct((k, n), BF16),
        compiler_params=_cp(("parallel",)),
        name="cast_weight",
    )(w)


def _na_table_kernel(rpb_ref, o_ref, *, n_ri, n_ci):
    h = pl.program_id(0)
    w = GRID_W
    qc = lax.broadcasted_iota(jnp.int32, (w, w), 0)
    kc = lax.broadcasted_iota(jnp.int32, (w, w), 1)
    ci = kc - qc + (NA_COLS - 1)
    qstart = jnp.clip(qc - NA_COLS // 2, 0, w - NA_COLS)
    valid = (kc >= qstart) & (kc < qstart + NA_COLS)
    neg = jnp.full((w, w), NEG, F32)
    tiles = []
    for ri in range(n_ri):
        t = jnp.zeros((w, w), F32)
        for j in range(n_ci):
            t = jnp.where(ci == j, rpb_ref[h * (n_ri * n_ci) + ri * n_ci + j], t)
        tiles.append(jnp.where(valid, t, neg))
    for e in range(NA_WIN):
        j0 = min(max(e - NA_ROWS // 2, 0), NA_WIN - NA_ROWS)
        for jp in range(NA_WIN // 2):
            pair = []
            for j in (2 * jp, 2 * jp + 1):
                pair.append(tiles[j - e + NA_ROWS - 1] if j0 <= j < j0 + NA_ROWS else neg)
            o_ref[0, e, :, jp * 2 * w:(jp + 1) * 2 * w] = jnp.concatenate(pair, axis=1)


def na_table(rpb):
    h, n_ri, n_ci = rpb.shape
    return pl.pallas_call(
        functools.partial(_na_table_kernel, n_ri=n_ri, n_ci=n_ci),
        grid=(h,),
        in_specs=[pl.BlockSpec(memory_space=pltpu.SMEM)],
        out_specs=pl.BlockSpec((1, NA_WIN, GRID_W, NA_WIN * GRID_W), lambda i: (i, 0, 0, 0)),
        out_shape=jax.ShapeDtypeStruct((h, NA_WIN, GRID_W, NA_WIN * GRID_W), F32),
        compiler_params=_cp(("parallel",)),
        name="na_table",
    )(rpb.reshape(-1).astype(F32))


def _na_kernel(q_ref, k_ref, v_ref, qg_ref, kg_ref, tab_ref, o_ref, qn_ref, kn_ref, *, rows):
    l = rows * GRID_W
    ch = _pick(l, (512,))

    def norm(src, g_ref, dst):
        for c in range(l // ch):
            x = src[0, c * ch:(c + 1) * ch, :].astype(F32)
            ms = jnp.mean(x * x, axis=-1, keepdims=True)
            dst[c * ch:(c + 1) * ch, :] = (x * lax.rsqrt(ms + EPS) * g_ref[...]).astype(BF16)

    norm(q_ref, qg_ref, qn_ref)
    norm(k_ref, kg_ref, kn_ref)
    scale = HEAD_DIM ** -0.5
    nq = NA_RBLK * GRID_W
    nkeys = NA_WIN * GRID_W

    def body(i, carry):
        rb = i * NA_RBLK
        kw = jnp.clip(rb - NA_ROWS // 2, 0, rows - NA_WIN)
        e0 = rb - kw
        q0 = pl.multiple_of(rb * GRID_W, nq)
        k0 = pl.multiple_of(kw * GRID_W, 256)
        q = qn_ref[pl.ds(q0, nq), :]
        kk = kn_ref[pl.ds(k0, nkeys), :]
        vv = v_ref[0, pl.ds(k0, nkeys), :]
        s = lax.dot_general(q, kk, (((1,), (1,)), ((), ())), preferred_element_type=F32) * scale
        s = s.reshape(NA_RBLK, GRID_W, nkeys) + tab_ref[0, pl.ds(e0, NA_RBLK)]
        m = jnp.max(s, axis=-1, keepdims=True)
        p = jnp.exp(s - m)
        den = jnp.sum(p, axis=-1, keepdims=True)
        o = jnp.dot(p.reshape(nq, nkeys).astype(BF16), vv, preferred_element_type=F32)
        o = o * (1.0 / den).reshape(nq, 1)
        o_ref[0, pl.ds(q0, nq), :] = o.astype(o_ref.dtype)
        return carry

    lax.fori_loop(0, rows // NA_RBLK, body, 0, unroll=2)


def na_attention(proj3, q_gain, k_gain, table, heads):
    b, l, _ = proj3.shape
    rows = l // GRID_W
    assert rows % NA_RBLK == 0 and rows >= NA_WIN
    blk = (1, l, HEAD_DIM)
    return pl.pallas_call(
        functools.partial(_na_kernel, rows=rows),
        grid=(heads, b),
        in_specs=[
            pl.BlockSpec(blk, lambda h, i: (i, 0, h)),
            pl.BlockSpec(blk, lambda h, i: (i, 0, heads + h)),
            pl.BlockSpec(blk, lambda h, i: (i, 0, 2 * heads + h)),
            pl.BlockSpec((1, HEAD_DIM), lambda h, i: (0, 0)),
            pl.BlockSpec((1, HEAD_DIM), lambda h, i: (0, 0)),
            pl.BlockSpec((1, NA_WIN, GRID_W, NA_WIN * GRID_W), lambda h, i: (h, 0, 0, 0)),
        ],
        out_specs=pl.BlockSpec(blk, lambda h, i: (i, 0, h)),
        out_shape=jax.ShapeDtypeStruct((b, l, heads * HEAD_DIM), BF16),
        scratch_shapes=[pltpu.VMEM((l, HEAD_DIM), BF16), pltpu.VMEM((l, HEAD_DIM), BF16)],
        compiler_params=_cp(("parallel", "parallel")),
        name="na_attention",
    )(proj3, proj3, proj3, q_gain.reshape(1, HEAD_DIM).astype(F32),
      k_gain.reshape(1, HEAD_DIM).astype(F32), table)


def _filt_kernel(w1t_ref, w1c_ref, w1s_ref, b1_ref, w2_ref, b2_ref, w3_ref, b3_ref, freq_ref, ld_ref,
                 f_ref, sum_ref, *, l, tl):
    i = pl.program_id(0)
    t = (i * tl + lax.broadcasted_iota(jnp.int32, (tl, 1), 0)).astype(F32) / l
    bands = (lax.broadcasted_iota(jnp.int32, (1, HY_BANDS), 1) + 1).astype(F32)
    ang = 2.0 * math.pi * t * bands
    dot = functools.partial(jnp.dot, precision=HIGHEST, preferred_element_type=F32)
    pre = t * w1t_ref[...] + dot(jnp.cos(ang), w1c_ref[...]) + dot(jnp.sin(ang), w1s_ref[...]) + b1_ref[...]
    freq = freq_ref[...]
    hid = jnp.sin(freq * pre)
    hid = jnp.sin(freq * (dot(hid, w2_ref[...]) + b2_ref[...]))
    f = dot(hid, w3_ref[...]) + b3_ref[...]
    f = f * jnp.exp(-jnp.exp(ld_ref[...]) * t)
    f_ref[...] = f
    part = jnp.sum(jnp.abs(f), axis=0, keepdims=True)

    @pl.when(i == 0)
    def _():
        sum_ref[...] = part

    @pl.when(i > 0)
    def _():
        sum_ref[...] += part


def _filt_combine_kernel(f_ref, sum_ref, f1_ref, f2_ref, ny_ref, *, c, tl):
    i = pl.program_id(0)
    tot = sum_ref[:, :c] + sum_ref[:, c:]
    inv = 1.0 / tot
    n = i * tl + lax.broadcasted_iota(jnp.int32, (tl, 1), 0)
    hf = f_ref[:, :c] * inv
    hb = jnp.where(n == 0, 0.0, f_ref[:, c:] * inv)
    f1 = hf + hb
    f1_ref[...] = f1.astype(f1_ref.dtype)
    f2_ref[...] = (hb - hf).astype(f2_ref.dtype)
    sign = (1 - 2 * (n & 1)).astype(F32)
    part = jnp.sum(f1 * sign, axis=0, keepdims=True)

    @pl.when(i == 0)
    def _():
        ny_ref[...] = part

    @pl.when(i > 0)
    def _():
        ny_ref[...] += part


def hyena_filters(l, w1, b1, w2, b2, w3, b3, freq, log_decay):
    hid = w2.shape[0]
    c2 = w3.shape[1]
    c = c2 // 2
    tl = _pick(l, (256,))
    row = lambda a: a.reshape(1, -1).astype(F32)
    full = lambda shape: pl.BlockSpec(shape, lambda i: (0, 0))
    f, s = pl.pallas_call(
        functools.partial(_filt_kernel, l=l, tl=tl),
        grid=(l // tl,),
        in_specs=[full((1, hid)), full((HY_BANDS, hid)), full((HY_BANDS, hid)), full((1, hid)),
                  full((hid, hid)), full((1, hid)), full((hid, c2)), full((1, c2)), full((1, hid)), full((1, c2))],
        out_specs=[pl.BlockSpec((tl, c2), lambda i: (i, 0)), full((1, c2))],
        out_shape=[jax.ShapeDtypeStruct((l, c2), F32), jax.ShapeDtypeStruct((1, c2), F32)],
        compiler_params=_cp(("arbitrary",)),
        name="hyena_filter_mlp",
    )(w1[0:1].astype(F32), w1[1:1 + HY_BANDS].astype(F32), w1[1 + HY_BANDS:].astype(F32), row(b1),
      w2.astype(F32), row(b2), w3.astype(F32), row(b3), row(freq), row(log_decay))
    return pl.pallas_call(
        functools.partial(_filt_combine_kernel, c=c, tl=tl),
        grid=(l // tl,),
        in_specs=[pl.BlockSpec((tl, c2), lambda i: (i, 0)), full((1, c2))],
        out_specs=[pl.BlockSpec((tl, c), lambda i: (i, 0)), pl.BlockSpec((tl, c), lambda i: (i, 0)), full((1, c))],
        out_shape=[jax.ShapeDtypeStruct((l, c), BF16), jax.ShapeDtypeStruct((l, c), BF16),
                   jax.ShapeDtypeStruct((1, c), F32)],
        compiler_params=_cp(("arbitrary",)),
        name="hyena_filter_combine",
    )(f, s)


def _conv_gate_kernel(z0_ref, z1_ref, z2_ref, w0_ref, w1_ref, w2_ref, b0_ref, b1_ref, b2_ref,
                      u_ref, x0_ref, ny_ref, *, l, ch):
    tc = z0_ref.shape[-1]
    ny = jnp.zeros((1, tc), F32)

    def conv(z_ref, w_ref, b_ref, s):
        z = z_ref[0, s:s + ch, :].astype(F32)
        rid = lax.broadcasted_iota(jnp.int32, (ch, tc), 0)
        prev = z_ref[0, s - 1:s, :].astype(F32) if s > 0 else jnp.zeros((1, tc), F32)
        nxt = z_ref[0, s + ch:s + ch + 1, :].astype(F32) if s + ch < l else jnp.zeros((1, tc), F32)
        up = jnp.where(rid == 0, prev, pltpu.roll(z, 1, 0))
        dn = jnp.where(rid == ch - 1, nxt, pltpu.roll(z, ch - 1, 0))
        return w_ref[0:1, :] * up + w_ref[1:2, :] * z + w_ref[2:3, :] * dn + b_ref[...]

    for s in range(0, l, ch):
        x0 = conv(z0_ref, w0_ref, b0_ref, s)
        x1 = conv(z1_ref, w1_ref, b1_ref, s)
        v = conv(z2_ref, w2_ref, b2_ref, s)
        u = (v * x1).astype(u_ref.dtype)
        u_ref[0, s:s + ch, :] = u
        x0_ref[0, s:s + ch, :] = x0.astype(x0_ref.dtype)
        n = s + lax.broadcasted_iota(jnp.int32, (ch, 1), 0)
        ny = ny + jnp.sum(u.astype(F32) * (1 - 2 * (n & 1)).astype(F32), axis=0, keepdims=True)
    ny_ref[0] = ny


def hyena_conv_gate(proj3, conv_w, conv_b, base, c):
    b, l, _ = proj3.shape
    tc = _pick(math.gcd(base, c), (256, 128))
    ch = _pick(l, (512,))
    nb0, nbc = base // tc, c // tc
    zspec = lambda part: pl.BlockSpec((1, l, tc), lambda i, j: (i, 0, nb0 + part * nbc + j))
    wspec = lambda part: pl.BlockSpec((3, tc), lambda i, j: (0, part * nbc + j))
    bspec = lambda part: pl.BlockSpec((1, tc), lambda i, j: (0, part * nbc + j))
    ospec = pl.BlockSpec((1, l, tc), lambda i, j: (i, 0, j))
    cw = conv_w.astype(F32)
    cb = conv_b.reshape(1, -1).astype(F32)
    return pl.pallas_call(
        functools.partial(_conv_gate_kernel, l=l, ch=ch),
        grid=(b, nbc),
        in_specs=[zspec(0), zspec(1), zspec(2), wspec(0), wspec(1), wspec(2), bspec(0), bspec(1), bspec(2)],
        out_specs=[ospec, ospec, pl.BlockSpec((1, 1, tc), lambda i, j: (i, 0, j))],
        out_shape=[jax.ShapeDtypeStruct((b, l, c), BF16), jax.ShapeDtypeStruct((b, l, c), BF16),
                   jax.ShapeDtypeStruct((b, 1, c), F32)],
        compiler_params=_cp(("parallel", "parallel")),
        name="hyena_conv_gate",
    )(proj3, proj3, proj3, cw, cw, cw, cb, cb, cb)


def _dft_table_kernel(cb_ref, sb_ref, ck_ref, sk_ref, c_ref, s_ref):
    cb, sb = cb_ref[...], sb_ref[...]
    ck, sk = ck_ref[0], sk_ref[0]
    c_ref[...] = (ck * cb - sk * sb).astype(c_ref.dtype)
    s_ref[...] = (sk * cb + ck * sb).astype(s_ref.dtype)


def dft_tables(l):
    tk = _pick(l, (256,))

    def angles(k, t):
        return ((k * t) % (2 * l)).astype(F32) * (math.pi / l)

    t = lax.broadcasted_iota(jnp.int32, (tk, l), 1)
    base = angles(lax.broadcasted_iota(jnp.int32, (tk, l), 0), t)
    t0 = lax.broadcasted_iota(jnp.int32, (l // tk, 1, l), 2)
    first = angles(lax.broadcasted_iota(jnp.int32, (l // tk, 1, l), 0) * tk, t0)
    full = pl.BlockSpec((tk, l), lambda i: (0, 0))
    row = pl.BlockSpec((1, 1, l), lambda i: (i, 0, 0))
    out = pl.BlockSpec((tk, l), lambda i: (i, 0))
    return pl.pallas_call(
        _dft_table_kernel,
        grid=(l // tk,),
        in_specs=[full, full, row, row],
        out_specs=[out, out],
        out_shape=[jax.ShapeDtypeStruct((l, l), BF16)] * 2,
        compiler_params=_cp(("parallel",)),
        name="dft_tables",
    )(jnp.cos(base), jnp.sin(base), jnp.cos(first), jnp.sin(first))


def _dft_fwd_kernel(c_ref, s_ref, u_ref, tre_ref, tim_ref, z1_ref, z2_ref, *, l, tk):
    kb = pl.program_id(0)
    u = u_ref[0]
    p = jnp.dot(c_ref[...], u, preferred_element_type=F32)
    q = jnp.dot(s_ref[...], u, preferred_element_type=F32)
    kidx = kb * tk + lax.broadcasted_iota(jnp.int32, (tk, 1), 0)
    sc = jnp.where(kidx == 0, 0.5 / l, 1.0 / l)
    tre = tre_ref[...] * sc
    tim = tim_ref[...] * sc
    z1_ref[0] = (p * tre + q * tim).astype(z1_ref.dtype)
    z2_ref[0] = (q * tre - p * tim).astype(z2_ref.dtype)


def _dft_inv_kernel(c_ref, s_ref, z1_ref, z2_ref, ny_ref, tny_ref, u_ref, x0_ref, hb_ref, o_ref, *, l, tt):
    tb = pl.program_id(2)
    y = jnp.dot(c_ref[...], z1_ref[0], preferred_element_type=F32)
    y = y + jnp.dot(s_ref[...], z2_ref[0], preferred_element_type=F32)
    tidx = tb * tt + lax.broadcasted_iota(jnp.int32, (tt, 1), 0)
    sign = (1 - 2 * (tidx & 1)).astype(F32)
    y = y + sign * (ny_ref[0] * tny_ref[...] * (0.5 / l))
    y = y + u_ref[0].astype(F32) * hb_ref[...]
    o_ref[0] = (y * x0_ref[0].astype(F32)).astype(o_ref.dtype)


def hyena_long_conv(u, ny, x0, cmat, smat, tre, tim, tny, hy_bias):
    b, l, c = u.shape
    tc = _pick(c, (512, 256, 128))
    tk = _pick(l, (512,))
    z1, z2 = pl.pallas_call(
        functools.partial(_dft_fwd_kernel, l=l, tk=tk),
        grid=(l // tk, b, c // tc),
        in_specs=[
            pl.BlockSpec((tk, l), lambda kb, i, j: (kb, 0)),
            pl.BlockSpec((tk, l), lambda kb, i, j: (kb, 0)),
            pl.BlockSpec((1, l, tc), lambda kb, i, j: (i, 0, j)),
            pl.BlockSpec((tk, tc), lambda kb, i, j: (kb, j)),
            pl.BlockSpec((tk, tc), lambda kb, i, j: (kb, j)),
        ],
        out_specs=[
            pl.BlockSpec((1, tk, tc), lambda kb, i, j: (i, kb, j)),
            pl.BlockSpec((1, tk, tc), lambda kb, i, j: (i, kb, j)),
        ],
        out_shape=[jax.ShapeDtypeStruct((b, l, c), BF16), jax.ShapeDtypeStruct((b, l, c), BF16)],
        compiler_params=_cp(("parallel", "parallel", "parallel")),
        name="hyena_dft_fwd",
    )(cmat, smat, u, tre, tim)
    tt = tk
    return pl.pallas_call(
        functools.partial(_dft_inv_kernel, l=l, tt=tt),
        grid=(b, c // tc, l // tt),
        in_specs=[
            pl.BlockSpec((tt, l), lambda i, j, tb: (tb, 0)),
            pl.BlockSpec((tt, l), lambda i, j, tb: (tb, 0)),
            pl.BlockSpec((1, l, tc), lambda i, j, tb: (i, 0, j)),
            pl.BlockSpec((1, l, tc), lambda i, j, tb: (i, 0, j)),
            pl.BlockSpec((1, 1, tc), lambda i, j, tb: (i, 0, j)),
            pl.BlockSpec((1, tc), lambda i, j, tb: (0, j)),
            pl.BlockSpec((1, tt, tc), lambda i, j, tb: (i, tb, j)),
            pl.BlockSpec((1, tt, tc), lambda i, j, tb: (i, tb, j)),
            pl.BlockSpec((1, tc), lambda i, j, tb: (0, j)),
        ],
        out_specs=pl.BlockSpec((1, tt, tc), lambda i, j, tb: (i, tb, j)),
        out_shape=jax.ShapeDtypeStruct((b, l, c), BF16),
        compiler_params=_cp(("parallel", "parallel", "arbitrary")),
        name="hyena_dft_inv",
    )(cmat, smat, z1, z2, ny, tny, u, x0, hy_bias.reshape(1, c).astype(F32))


def _cmul(ar, ai, br, bi):
    return ar * br - ai * bi, ar * bi + ai * br


def _s5_prep_kernel(lre_ref, lim_ref, ls_ref, bre_ref, bim_ref, cre_ref, cim_ref,
                    pwr_ref, pwi_ref, wr_ref, wi_ref, clr_ref, cli_ref, k_ref):
    cg = bre_ref.shape[1]
    lr = jnp.minimum(lre_ref[...], -1e-4)
    li = lim_ref[...]
    step = jnp.exp(ls_ref[...])
    mag = jnp.exp(lr * step)
    ar = mag * jnp.cos(li * step)
    ai = mag * jnp.sin(li * step)
    den = lr * lr + li * li
    nr = ar - 1.0
    qr = (nr * lr + ai * li) / den
    qi = (ai * lr - nr * li) / den
    bbr, bbi = _cmul(qr, qi, bre_ref[...], bim_ref[...])
    cr = cre_ref[...]
    ci = cim_ref[...]
    pr = jnp.ones_like(ar)
    pi = jnp.zeros_like(ar)
    for tau in range(S5_T):
        sl = slice(tau * cg, (tau + 1) * cg)
        wr, wi = _cmul(pr, pi, bbr, bbi)
        wr_ref[:, sl, :] = wr
        wi_ref[:, sl, :] = wi
        pr, pi = _cmul(pr, pi, ar, ai)
        clr, cli = _cmul(cr, ci, pr, pi)
        clr_ref[:, sl, :] = clr
        cli_ref[:, sl, :] = cli
    qr, qi = pr, pi
    for m in range(8):
        pwr_ref[:, m:m + 1, :] = qr
        pwi_ref[:, m:m + 1, :] = qi
        qr, qi = _cmul(qr, qi, pr, pi)
    dims = (((2,), (2,)), ((0,), (0,)))
    k_ref[...] = (lax.dot_general(cr, wr_ref[...], dims, precision=HIGHEST, preferred_element_type=F32)
                  - lax.dot_general(ci, wi_ref[...], dims, precision=HIGHEST, preferred_element_type=F32))


def s5_prep(lam_re, lam_im, log_step, b_re, b_im, c_re, c_im):
    _, g, p = lam_re.shape
    cg = b_re.shape[-1]
    n = 2 * g
    tg = _pick(n, (8,))
    flat = lambda a: a.reshape(n, 1, p).astype(F32)
    bt = lambda a: a.reshape(n, p, cg).transpose(0, 2, 1).astype(F32)
    ct = lambda a: a.reshape(n, cg, p).astype(F32)
    s1 = pl.BlockSpec((tg, 1, p), lambda i: (i, 0, 0))
    s3 = pl.BlockSpec((tg, cg, p), lambda i: (i, 0, 0))
    s8 = pl.BlockSpec((tg, 8, p), lambda i: (i, 0, 0))
    st = pl.BlockSpec((tg, S5_T * cg, p), lambda i: (i, 0, 0))
    sk = pl.BlockSpec((tg, cg, S5_T * cg), lambda i: (i, 0, 0))
    big = jax.ShapeDtypeStruct((n, S5_T * cg, p), F32)
    return pl.pallas_call(
        _s5_prep_kernel,
        grid=(n // tg,),
        in_specs=[s1, s1, pl.BlockSpec((tg, 1, 1), lambda i: (i, 0, 0)), s3, s3, s3, s3],
        out_specs=[s8, s8, st, st, st, st, sk],
        out_shape=[jax.ShapeDtypeStruct((n, 8, p), F32)] * 2 + [big] * 4
                  + [jax.ShapeDtypeStruct((n, cg, S5_T * cg), F32)],
        compiler_params=_cp(("parallel",)),
        name="s5_discretise",
    )(flat(lam_re), flat(lam_im), log_step.reshape(n, 1, 1).astype(F32), bt(b_re), bt(b_im), ct(c_re), ct(c_im))


def _gelu(x):
    return 0.5 * x * (1.0 + jnp.tanh(math.sqrt(2.0 / math.pi) * (x + 0.044715 * (x * x * x))))


def _s5_kernel(u_ref, k_ref, b_ref, c_ref, tab_ref, o_ref, x_ref, *, s_sub, gt):
    nblk = u_ref.shape[1] // 8
    ns = x_ref.shape[-1] // 2
    half = ns // 2
    nsteps = (8 // s_sub).bit_length() - 1
    for g in range(gt):
        x_ref[g] = jnp.dot(u_ref[g], b_ref[g], preferred_element_type=F32)
    row = lax.broadcasted_iota(jnp.int32, (8, ns), 0)

    def scan_block(g, vr, vi, cr, ci, reverse):
        for k in range(nsteps):
            sh = s_sub << k
            shift = 8 - sh if reverse else sh
            ar, ai = tab_ref[g, 1 + k, :, :ns], tab_ref[g, 1 + k, :, ns:]
            dr, di = _cmul(ar, ai, pltpu.roll(vr, shift, 0), pltpu.roll(vi, shift, 0))
            vr, vi = vr + dr, vi + di
        if s_sub == 8:
            tr, ti = cr, ci
        else:
            src = (row < s_sub) if reverse else (row >= 8 - s_sub)
            tr = jnp.where(src, cr, 0.0)
            ti = jnp.where(src, ci, 0.0)
            for k in range(nsteps):
                tr = tr + pltpu.roll(tr, s_sub << k, 0)
                ti = ti + pltpu.roll(ti, s_sub << k, 0)
        dr, di = _cmul(tab_ref[g, 0, :, :ns], tab_ref[g, 0, :, ns:], tr, ti)
        fr, fi = vr + dr, vi + di
        if s_sub == 8:
            return tr, ti, fr, fi
        edge = (row >= 8 - s_sub) if reverse else (row < s_sub)
        shift = 8 - s_sub if reverse else s_sub
        xr = jnp.where(edge, tr, pltpu.roll(fr, shift, 0))
        xi = jnp.where(edge, ti, pltpu.roll(fi, shift, 0))
        return xr, xi, fr, fi

    def body(i, carry):
        of = pl.multiple_of(i * 8, 8)
        ob = pl.multiple_of((nblk - 1 - i) * 8, 8)
        new = []
        for g in range(gt):
            cfr, cfi, cbr, cbi = carry[4 * g:4 * g + 4]
            xr, xi, cfr, cfi = scan_block(g, x_ref[g, pl.ds(of, 8), :ns], x_ref[g, pl.ds(of, 8), ns:],
                                          cfr, cfi, False)
            x_ref[g, pl.ds(of, 8), :half] = xr[:, :half]
            x_ref[g, pl.ds(of, 8), ns:ns + half] = xi[:, :half]
            xr, xi, cbr, cbi = scan_block(g, x_ref[g, pl.ds(ob, 8), :ns], x_ref[g, pl.ds(ob, 8), ns:],
                                          cbr, cbi, True)
            x_ref[g, pl.ds(ob, 8), half:ns] = xr[:, half:]
            x_ref[g, pl.ds(ob, 8), ns + half:] = xi[:, half:]
            new += [cfr, cfi, cbr, cbi]
        return tuple(new)

    zero = jnp.zeros((8, ns), F32)
    lax.fori_loop(0, nblk, body, (zero,) * (4 * gt), unroll=2)
    for g in range(gt):
        y = jnp.dot(u_ref[g], k_ref[g], preferred_element_type=F32)
        y = y + jnp.dot(x_ref[g].astype(BF16), c_ref[g], preferred_element_type=F32)
        o_ref[g] = _gelu(y).astype(o_ref.dtype)


def s5_weights(lam_re, lam_im, log_step, b_re, b_im, c_re, c_im, d_skip):
    _, g, p = lam_re.shape
    cg = b_re.shape[-1]
    t = S5_T
    pwr, pwi, wr, wi, clr, cli, kk = s5_prep(lam_re, lam_im, log_step, b_re, b_im, c_re, c_im)
    w4 = lambda a: a.reshape(2, g, t, cg, p)
    wr, wi, clr, cli = w4(wr), w4(wi), w4(clr), w4(cli)
    bm = [wr[0, :, ::-1], wr[1], wi[0, :, ::-1], wi[1]]
    bmat = jnp.concatenate([a.reshape(g, t * cg, p) for a in bm], axis=-1).astype(BF16)
    to_rows = lambda a: a.transpose(0, 3, 1, 2).reshape(g, p, t * cg)
    cm = [to_rows(clr[0]), to_rows(clr[1, :, ::-1]), -to_rows(cli[0]), -to_rows(cli[1, :, ::-1])]
    cmat = jnp.concatenate(cm, axis=1).astype(BF16)
    kk = kk.reshape(2, g, cg, t, cg)
    ts = jnp.arange(t)
    lag = ts[None, :] - ts[:, None]
    kf = jnp.where((lag >= 0)[None, None, :, :, None], kk[0][:, :, jnp.clip(lag, 0, t - 1), :], 0.0)
    kb = jnp.where((lag <= 0)[None, None, :, :, None], kk[1][:, :, jnp.clip(-lag, 0, t - 1), :], 0.0)
    skip = (d_skip.astype(F32).reshape(g, cg)[:, :, None, None, None]
            * jnp.eye(t, dtype=F32)[None, None, :, :, None] * jnp.eye(cg, dtype=F32)[None, :, None, None, :])
    kmat = (kf + kb + skip).transpose(0, 2, 4, 3, 1).reshape(g, t * cg, t * cg).astype(BF16)
    return kmat, bmat, cmat, (pwr.reshape(2, g, 8, p), pwi.reshape(2, g, 8, p))


def s5_scan_tables(pw, s_sub):
    pwr, pwi = pw
    nsteps = (8 // s_sub).bit_length() - 1
    rows = jnp.arange(8)
    tabs = []

    def entry(fwd_idx, bwd_idx, fwd_mask, bwd_mask):
        cols = [pwr[0][:, fwd_idx] * fwd_mask[None, :, None], pwr[1][:, bwd_idx] * bwd_mask[None, :, None],
                pwi[0][:, fwd_idx] * fwd_mask[None, :, None], pwi[1][:, bwd_idx] * bwd_mask[None, :, None]]
        return jnp.concatenate(cols, axis=-1)

    ones = jnp.ones((8,), F32)
    tabs.append(entry(rows // s_sub, (7 - rows) // s_sub, ones, ones))
    for k in range(nsteps):
        sh = s_sub << k
        idx = jnp.full((8,), (1 << k) - 1)
        tabs.append(entry(idx, idx, (rows >= sh).astype(F32), (rows < 8 - sh).astype(F32)))
    return jnp.stack(tabs, axis=1)


def _pack2(a, b):
    ua = lax.bitcast_convert_type(a.astype(BF16).astype(F32), jnp.int32)
    ub = lax.bitcast_convert_type(b.astype(BF16).astype(F32), jnp.int32)
    return ua | lax.shift_right_logical(ub, 16)


def _unpack2(p):
    a = lax.bitcast_convert_type(p & jnp.int32(-65536), F32)
    b = lax.bitcast_convert_type(lax.shift_left(p, 16), F32)
    return a, b


def _block_transpose(x):
    n = x.shape[1]
    rows = lax.broadcasted_iota(jnp.int32, x.shape, 0)
    lanes = lax.broadcasted_iota(jnp.int32, x.shape, 1)
    for k in range(4):
        s = 1 << k
        rbit = (rows >> k) & 1
        lbit = (lanes >> (4 + k)) & 1
        up = pltpu.roll(pltpu.roll(x, s, 0), n - 16 * s, 1)
        dn = pltpu.roll(pltpu.roll(x, x.shape[0] - s, 0), 16 * s, 1)
        x = jnp.where((rbit == 1) & (lbit == 0), up, jnp.where((rbit == 0) & (lbit == 1), dn, x))
    return x


def _s5_pack_kernel(x_ref, g_ref, o_ref, lo_ref, hi_ref, *, nb):
    d = x_ref.shape[1]
    hw = lo_ref.shape[1]
    w = 2 * hw
    nj = lo_ref.shape[0] // 16
    if nb is not None:
        live = pl.program_id(0) < nb
    x = x_ref[...]
    inv = lax.rsqrt(jnp.mean(x * x, axis=-1, keepdims=True) + EPS)
    for pair in range(d // (2 * w)):
        xs = []
        for blk in (2 * pair, 2 * pair + 1):
            sl = slice(blk * w, (blk + 1) * w)
            xn = x_ref[:, sl] * inv * g_ref[:, sl]
            if nb is not None:
                xn = jnp.where(live, xn, 0.0)
            xs.append(xn)
        t = _block_transpose(_pack2(*xs))
        lo_ref[...] = t[:, :hw]
        hi_ref[...] = t[:, hw:]
        for gl in range(16):
            lo = _unpack2(lo_ref[pl.ds(gl, nj, stride=16), :])
            hi = _unpack2(hi_ref[pl.ds(gl, nj, stride=16), :])
            for q in range(2):
                o_ref[(2 * pair + q) * 16 + gl, :, :hw] = lo[q].astype(o_ref.dtype)
                o_ref[(2 * pair + q) * 16 + gl, :, hw:] = hi[q].astype(o_ref.dtype)


def _s5_unpack_kernel(y_ref, o_ref, lo_ref, hi_ref):
    hw = lo_ref.shape[1]
    w = 2 * hw
    nj = lo_ref.shape[0] // 16
    for pair in range(o_ref.shape[1] // (2 * w)):
        for gl in range(16):
            yg = _pack2(y_ref[2 * pair * 16 + gl].astype(F32), y_ref[(2 * pair + 1) * 16 + gl].astype(F32))
            lo_ref[gl * nj:(gl + 1) * nj, :] = yg[:, :hw]
            hi_ref[gl * nj:(gl + 1) * nj, :] = yg[:, hw:]
        v = jnp.concatenate(
            [jnp.concatenate([lo_ref[pl.ds(j, 16, stride=nj), :], hi_ref[pl.ds(j, 16, stride=nj), :]], axis=1)
             for j in range(nj)], axis=0)
        t = _unpack2(_block_transpose(v))
        for q in range(2):
            o_ref[:, (2 * pair + q) * w:(2 * pair + q + 1) * w] = t[q].astype(o_ref.dtype)


def s5_mixer_core(x, gain, b, l, s5w, gt=4):
    d = x.shape[1]
    kmat, bmat, cmat, pw = s5w
    g = kmat.shape[0]
    cg = d // g
    t = S5_T
    nc = l // t
    nj = 16
    w = t * cg
    s_sub = 1 << (b - 1).bit_length()
    assert s_sub <= 8 and (nc * s_sub) % 8 == 0 and g % gt == 0 and cg == 16 and t == 16 and nc % nj == 0
    tab = s5_scan_tables(pw, s_sub)
    tok = nj * t
    lb = l // tok
    u = pl.pallas_call(
        functools.partial(_s5_pack_kernel, nb=None if s_sub == b else b),
        grid=(s_sub, lb),
        in_specs=[pl.BlockSpec((tok, d), lambda i, j: (jnp.minimum(i, b - 1) * lb + j, 0)),
                  pl.BlockSpec((1, d), lambda i, j: (0, 0))],
        out_specs=pl.BlockSpec((g, nj, w), lambda i, j: (0, j, i)),
        out_shape=jax.ShapeDtypeStruct((g, nc, s_sub * w), BF16),
        scratch_shapes=[pltpu.VMEM((nj * 16, w // 2), jnp.int32)] * 2,
        compiler_params=_cp(("parallel", "parallel")),
        name="s5_pack",
    )(x, gain.reshape(1, d).astype(F32))
    u = u.reshape(g, nc * s_sub, w)
    r = nc * s_sub
    ns4 = bmat.shape[-1]
    gspec = lambda shape: pl.BlockSpec((gt,) + shape, lambda i: (i,) + (0,) * len(shape))
    y = pl.pallas_call(
        functools.partial(_s5_kernel, s_sub=s_sub, gt=gt),
        grid=(g // gt,),
        in_specs=[gspec((r, w)), gspec((w, w)), gspec((w, ns4)), gspec((ns4, w)), gspec((tab.shape[1], 8, ns4))],
        out_specs=gspec((r, w)),
        out_shape=jax.ShapeDtypeStruct((g, r, w), BF16),
        scratch_shapes=[pltpu.VMEM((gt, r, ns4), F32)],
        compiler_params=_cp(("parallel",)),
        name="s5_chunked",
    )(u, kmat, bmat, cmat, tab)
    return pl.pallas_call(
        _s5_unpack_kernel,
        grid=(b, lb),
        in_specs=[pl.BlockSpec((g, nj, w), lambda i, j: (0, j, i))],
        out_specs=pl.BlockSpec((tok, d), lambda i, j: (i * lb + j, 0)),
        out_shape=jax.ShapeDtypeStruct((b * l, d), BF16),
        scratch_shapes=[pltpu.VMEM((nj * 16, w // 2), jnp.int32)] * 2,
        compiler_params=_cp(("parallel", "parallel")),
        name="s5_unpack",
    )(y.reshape(g, nc, s_sub * w))


def _xattn_kernel(q_ref, k_ref, v_ref, qg_ref, kg_ref, o_ref, *, heads):
    scale = HEAD_DIM ** -0.5

    def hnorm(x, g_ref):
        ms = jnp.mean(x * x, axis=-1, keepdims=True)
        return (x * lax.rsqrt(ms + EPS) * g_ref[...]).astype(BF16)

    for h in range(heads):
        sl = slice(h * HEAD_DIM, (h + 1) * HEAD_DIM)
        q = hnorm(q_ref[0, :, sl], qg_ref)
        k = hnorm(k_ref[0, :, sl], kg_ref)
        s = lax.dot_general(q, k, (((1,), (1,)), ((), ())), preferred_element_type=F32) * scale
        m = jnp.max(s, axis=-1, keepdims=True)
        p = jnp.exp(s - m)
        den = jnp.sum(p, axis=-1, keepdims=True)
        o = jnp.dot(p.astype(BF16), v_ref[0, :, sl].astype(BF16), preferred_element_type=F32)
        o_ref[0, :, sl] = (o * (1.0 / den)).astype(o_ref.dtype)


def cross_attention_core(q, k, v, q_gain, k_gain):
    b, l, xw = q.shape
    mm = k.shape[1]
    tq = _pick(l, (512, 256, 128))
    g = lambda a: a.reshape(1, HEAD_DIM).astype(F32)
    return pl.pallas_call(
        functools.partial(_xattn_kernel, heads=xw // HEAD_DIM),
        grid=(b, l // tq),
        in_specs=[
            pl.BlockSpec((1, tq, xw), lambda i, j: (i, j, 0)),
            pl.BlockSpec((1, mm, xw), lambda i, j: (i, 0, 0)),
            pl.BlockSpec((1, mm, xw), lambda i, j: (i, 0, 0)),
            pl.BlockSpec((1, HEAD_DIM), lambda i, j: (0, 0)),
            pl.BlockSpec((1, HEAD_DIM), lambda i, j: (0, 0)),
        ],
        out_specs=pl.BlockSpec((1, tq, xw), lambda i, j: (i, j, 0)),
        out_shape=jax.ShapeDtypeStruct((b, l, xw), BF16),
        compiler_params=_cp(("parallel", "parallel")),
        name="cross_attention",
    )(q, k, v, g(q_gain), g(k_gain))


def kernel(x_prompt, x_sample, mem_prompt, mem_sample, g_mix, g_cross, g_mem, g_ffn, w_in, na_q_gain, na_k_gain, na_rpb, hy_conv_w, hy_conv_b, hy_f_w1, hy_f_b1, hy_f_w2, hy_f_b2, hy_f_w3, hy_f_b3, hy_f_freq, hy_log_decay, hy_bias, w_out, s5_lam_re, s5_lam_im, s5_log_step, s5_b_re, s5_b_im, s5_c_re, s5_c_im, s5_d, w_glu, x_wq, x_wk, x_wv, x_wo, x_q_gain, x_k_gain, w_ffn_gate, w_ffn_up, w_ffn_down):
    depth, d = g_mix.shape
    heads = na_rpb.shape[1]
    naw = heads * HEAD_DIM
    hyw = hy_bias.shape[-1]
    bf = lambda w: w.astype(BF16)

    layers = []
    for layer in range(depth):
        i = layer // 2
        p = {}
        if layer % 2 == 0:
            p["w_in"] = cast_weight(w_in, i)
            p["w_out"] = cast_weight(w_out, i)
            p["table"] = na_table(na_rpb[i])
            p["filters"] = {}
        else:
            p["s5w"] = s5_weights(s5_lam_re[i], s5_lam_im[i], s5_log_step[i], s5_b_re[i], s5_b_im[i],
                                  s5_c_re[i], s5_c_im[i], s5_d[i])
            p["w_glu"] = cast_weight(w_glu, i)
        p["wq"], p["wk"] = cast_weight(x_wq, layer), cast_weight(x_wk, layer)
        p["wv"], p["wo"] = cast_weight(x_wv, layer), cast_weight(x_wo, layer)
        p["ffn"] = (cast_weight(w_ffn_gate, layer), cast_weight(w_ffn_up, layer), cast_weight(w_ffn_down, layer))
        layers.append(p)
    tables = {}

    def filters_for(layer, l):
        p = layers[layer]
        if l not in p["filters"]:
            i = layer // 2
            if l not in tables:
                tables[l] = dft_tables(l)
            cmat, smat = tables[l]
            f1, f2, ny = hyena_filters(l, hy_f_w1[i], hy_f_b1[i], hy_f_w2[i], hy_f_b2[i], hy_f_w3[i],
                                       hy_f_b3[i], hy_f_freq[i], hy_log_decay[i].reshape(-1))
            tre = matmul([cmat], [(f1, 0)], f1.shape[1])
            tim = matmul([smat], [(f2, 0)], f2.shape[1])
            p["filters"][l] = (cmat, smat, tre, tim, ny)
        return p["filters"][l]

    def run(x, mem):
        b, l, _ = x.shape
        nm = mem.shape[1]
        x = x.reshape(b * l, d)
        mem2 = mem.reshape(b * nm, d)
        for layer in range(depth):
            i = layer // 2
            p = layers[layer]
            if layer % 2 == 0:
                xn = rmsnorm(x, g_mix[layer], BF16)
                proj = matmul([xn], [(p["w_in"], 0)], p["w_in"].shape[1], out_dtype=BF16)
                proj3 = proj.reshape(b, l, -1)
                y_a = na_attention(proj3, na_q_gain[i], na_k_gain[i], p["table"], heads)
                cmat, smat, tre, tim, tny = filters_for(layer, l)
                u, x0, ny = hyena_conv_gate(proj3, hy_conv_w[i], hy_conv_b[i], 3 * naw, hyw)
                y_b = hyena_long_conv(u, ny, x0, cmat, smat, tre, tim, tny, hy_bias[i])
                y_a, y_b = y_a.reshape(b * l, naw), y_b.reshape(b * l, hyw)
                ymix = [y_a, y_b] if naw == hyw else [jnp.concatenate([y_a, y_b], axis=-1)]
                x = matmul(ymix, [(p["w_out"], 0)], d, epi="res", res=x)
            else:
                y = s5_mixer_core(x, g_mix[layer], b, l, p["s5w"])
                x = matmul([y], [(p["w_glu"], 0), (p["w_glu"], d)], d, epi="glu_res", res=x)
            mn = rmsnorm(mem2, g_mem[layer], BF16)
            xw = p["wq"].shape[1]
            q = matmul([x], [(p["wq"], 0)], xw, norm_gain=g_cross[layer])
            k = matmul([mn], [(p["wk"], 0)], xw)
            v = matmul([mn], [(p["wv"], 0)], xw)
            o = cross_attention_core(q.reshape(b, l, xw), k.reshape(b, nm, xw), v.reshape(b, nm, xw),
                                     x_q_gain[layer], x_k_gain[layer])
            x = matmul([o.reshape(b * l, xw)], [(p["wo"], 0)], d, epi="res", res=x)
            wg, wu, wd = p["ffn"]
            xn = rmsnorm(x, g_ffn[layer], BF16)
            hdn = matmul([xn], [(wg, 0), (wu, 0)], wg.shape[1], epi="swiglu", out_dtype=BF16)
            x = matmul([hdn], [(wd, 0)], d, epi="res", res=x)
        return x.reshape(b, l, d)

    return run(x_prompt, mem_prompt), run(x_sample, mem_sample)
```

```python
import functools
import math

import jax
import jax.numpy as jnp
from jax import lax
from jax.experimental import pallas as pl
from jax.experimental.pallas import tpu as pltpu

F32 = jnp.float32
BF16 = jnp.bfloat16
EPS = 1e-6
HEAD_DIM = 128
GRID_W = 64
NA_ROWS = 8
NA_COLS = 16
NA_RBLK = 8
NA_WIN = 16
HY_BANDS = 16
S5_GROUP = 16
S5_T = 16
NEG = -1e30
HIGHEST = lax.Precision.HIGHEST
VMEM_LIMIT = 56 * 1024 * 1024


def _cp(sem, vmem=VMEM_LIMIT):
    return pltpu.CompilerParams(dimension_semantics=sem, vmem_limit_bytes=vmem)


def _pick(n, cands):
    for c in cands:
        if c <= n and n % c == 0:
            return c
    return n


def _sigmoid(x):
    return 1.0 / (1.0 + jnp.exp(-x))


def _rmsnorm_kernel(x_ref, g_ref, o_ref):
    x = x_ref[...].astype(F32)
    ms = jnp.mean(x * x, axis=-1, keepdims=True)
    o_ref[...] = (x * lax.rsqrt(ms + EPS) * g_ref[...]).astype(o_ref.dtype)


def rmsnorm(x, g, out_dtype):
    m, d = x.shape
    tm = _pick(m, (256, 128, 64, 32, 16, 8))
    return pl.pallas_call(
        _rmsnorm_kernel,
        grid=(m // tm,),
        in_specs=[pl.BlockSpec((tm, d), lambda i: (i, 0)), pl.BlockSpec((1, d), lambda i: (0, 0))],
        out_specs=pl.BlockSpec((tm, d), lambda i: (i, 0)),
        out_shape=jax.ShapeDtypeStruct((m, d), out_dtype),
        compiler_params=_cp(("parallel",)),
        name="rmsnorm",
    )(x, g.reshape(1, d).astype(F32))


def _mm_kernel(*refs, na, nb, epi, norm):
    a_refs = refs[:na]
    b_refs = refs[na:na + na * nb]
    pos = na + na * nb
    res_ref = None
    if epi in ("res", "glu_res"):
        res_ref = refs[pos]
        pos += 1
    if norm:
        g_ref, o_ref, xn_ref = refs[pos], refs[pos + 1], refs[pos + 2]

        @pl.when(pl.program_id(1) == 0)
        def _():
            rows = xn_ref.shape[0]
            ch = _pick(rows, (256,))
            for s in range(0, rows, ch):
                x = a_refs[0][s:s + ch, :]
                inv = lax.rsqrt(jnp.mean(x * x, axis=-1, keepdims=True) + EPS)
                xn_ref[s:s + ch, :] = (x * inv * g_ref[...]).astype(xn_ref.dtype)

        avals = [xn_ref[...]]
    else:
        o_ref = refs[pos]
        avals = [a[...] for a in a_refs]
    parts = []
    for q in range(nb):
        acc = None
        for p, a in enumerate(avals):
            d = jnp.dot(a, b_refs[q * na + p][...], preferred_element_type=F32)
            acc = d if acc is None else acc + d
        parts.append(acc)
    if epi == "plain":
        out = parts[0]
    elif epi == "res":
        out = res_ref[...] + parts[0]
    elif epi == "swiglu":
        out = parts[0] * _sigmoid(parts[0]) * parts[1]
    else:
        out = res_ref[...] + parts[0] * _sigmoid(parts[1])
    o_ref[...] = out.astype(o_ref.dtype)


def matmul(a_parts, bs, n, *, epi="plain", res=None, out_dtype=F32, norm_gain=None):
    m, kp = a_parts[0].shape
    na = len(a_parts)
    k = kp * na
    wide = k <= 4096
    norm = norm_gain is not None
    tall = wide and res is None and out_dtype == BF16
    tm = _pick(m, ((2048,) if tall else ()) + ((1024,) if wide else ()) + (512, 256, 128, 64, 32, 16, 8))
    tn = _pick(n, (512, 256, 128))
    if norm:
        tm = _pick(m, (512, 256, 128, 64, 32, 16, 8))
    in_specs = [pl.BlockSpec((tm, kp), lambda i, j: (i, 0)) for _ in a_parts]
    args = list(a_parts)
    for b, off in bs:
        assert off % tn == 0 and b.shape[0] == k
        for p in range(na):
            in_specs.append(pl.BlockSpec((kp, tn), functools.partial(lambda i, j, p, ob: (p, ob + j), p=p, ob=off // tn)))
            args.append(b)
    if res is not None:
        in_specs.append(pl.BlockSpec((tm, tn), lambda i, j: (i, j)))
        args.append(res)
    if norm:
        assert na == 1
        in_specs.append(pl.BlockSpec((1, k), lambda i, j: (0, 0)))
        args.append(norm_gain.reshape(1, k).astype(F32))
    return pl.pallas_call(
        functools.partial(_mm_kernel, na=na, nb=len(bs), epi=epi, norm=norm),
        grid=(m // tm, n // tn),
        in_specs=in_specs,
        out_specs=pl.BlockSpec((tm, tn), lambda i, j: (i, j)),
        out_shape=jax.ShapeDtypeStruct((m, n), out_dtype),
        scratch_shapes=[pltpu.VMEM((tm, k), BF16)] if norm else [],
        compiler_params=_cp(("parallel", "arbitrary" if norm else "parallel")),
        name=("mm_norm_" if norm else "mm_") + epi,
    )(*args)


def _cast_kernel(w_ref, o_ref):
    o_ref[...] = w_ref[0].astype(o_ref.dtype)


def cast_weight(w, layer):
    _, k, n = w.shape
    tk = _pick(k, (128, 64, 32, 16))
    return pl.pallas_call(
        _cast_kernel,
        grid=(k // tk,),
        in_specs=[pl.BlockSpec((1, tk, n), lambda i: (layer, i, 0))],
        out_specs=pl.BlockSpec((tk, n), lambda i: (i, 0)),
        out_shape=jax.ShapeDtypeStruct((k, n), BF16),
        compiler_params=_cp(("parallel",)),
        name="cast_weight",
    )(w)


def _na_table_kernel(rpb_ref, o_ref, *, n_ri, n_ci):
    h = pl.program_id(0)
    w = GRID_W
    qc = lax.broadcasted_iota(jnp.int32, (w, w), 0)
    kc = lax.broadcasted_iota(jnp.int32, (w, w), 1)
    ci = kc - qc + (NA_COLS - 1)
    qstart = jnp.clip(qc - NA_COLS // 2, 0, w - NA_COLS)
    valid = (kc >= qstart) & (kc < qstart + NA_COLS)
    neg = jnp.full((w, w), NEG, F32)
    tiles = []
    for ri in range(n_ri):
        t = jnp.zeros((w, w), F32)
        for j in range(n_ci):
            t = jnp.where(ci == j, rpb_ref[h * (n_ri * n_ci) + ri * n_ci + j], t)
        tiles.append(jnp.where(valid, t, neg))
    for e in range(NA_WIN):
        j0 = min(max(e - NA_ROWS // 2, 0), NA_WIN - NA_ROWS)
        for jp in range(NA_WIN // 2):
            pair = []
            for j in (2 * jp, 2 * jp + 1):
                pair.append(tiles[j - e + NA_ROWS - 1] if j0 <= j < j0 + NA_ROWS else neg)
            o_ref[0, e, :, jp * 2 * w:(jp + 1) * 2 * w] = jnp.concatenate(pair, axis=1)


def na_table(rpb):
    h, n_ri, n_ci = rpb.shape
    return pl.pallas_call(
        functools.partial(_na_table_kernel, n_ri=n_ri, n_ci=n_ci),
        grid=(h,),
        in_specs=[pl.BlockSpec(memory_space=pltpu.SMEM)],
        out_specs=pl.BlockSpec((1, NA_WIN, GRID_W, NA_WIN * GRID_W), lambda i: (i, 0, 0, 0)),
        out_shape=jax.ShapeDtypeStruct((h, NA_WIN, GRID_W, NA_WIN * GRID_W), F32),
        compiler_params=_cp(("parallel",)),
        name="na_table",
    )(rpb.reshape(-1).astype(F32))


def _na_kernel(q_ref, k_ref, v_ref, qg_ref, kg_ref, tab_ref, o_ref, qn_ref, kn_ref, *, rows):
    l = rows * GRID_W
    ch = _pick(l, (512,))

    def norm(src, g_ref, dst):
        for c in range(l // ch):
            x = src[0, c * ch:(c + 1) * ch, :].astype(F32)
            ms = jnp.mean(x * x, axis=-1, keepdims=True)
            dst[c * ch:(c + 1) * ch, :] = (x * lax.rsqrt(ms + EPS) * g_ref[...]).astype(BF16)

    norm(q_ref, qg_ref, qn_ref)
    norm(k_ref, kg_ref, kn_ref)
    scale = HEAD_DIM ** -0.5
    nq = NA_RBLK * GRID_W
    nkeys = NA_WIN * GRID_W

    def body(i, carry):
        rb = i * NA_RBLK
        kw = jnp.clip(rb - NA_ROWS // 2, 0, rows - NA_WIN)
        e0 = rb - kw
        q0 = pl.multiple_of(rb * GRID_W, nq)
        k0 = pl.multiple_of(kw * GRID_W, 256)
        q = qn_ref[pl.ds(q0, nq), :]
        kk = kn_ref[pl.ds(k0, nkeys), :]
        vv = v_ref[0, pl.ds(k0, nkeys), :]
        s = lax.dot_general(q, kk, (((1,), (1,)), ((), ())), preferred_element_type=F32) * scale
        s = s.reshape(NA_RBLK, GRID_W, nkeys) + tab_ref[0, pl.ds(e0, NA_RBLK)]
        m = jnp.max(s, axis=-1, keepdims=True)
        p = jnp.exp(s - m)
        den = jnp.sum(p, axis=-1, keepdims=True)
        o = jnp.dot(p.reshape(nq, nkeys).astype(BF16), vv, preferred_element_type=F32)
        o = o * (1.0 / den).reshape(nq, 1)
        o_ref[0, pl.ds(q0, nq), :] = o.astype(o_ref.dtype)
        return carry

    lax.fori_loop(0, rows // NA_RBLK, body, 0, unroll=4)


def na_attention(proj3, q_gain, k_gain, table, heads):
    b, l, _ = proj3.shape
    rows = l // GRID_W
    assert rows % NA_RBLK == 0 and rows >= NA_WIN
    blk = (1, l, HEAD_DIM)
    return pl.pallas_call(
        functools.partial(_na_kernel, rows=rows),
        grid=(heads, b),
        in_specs=[
            pl.BlockSpec(blk, lambda h, i: (i, 0, h)),
            pl.BlockSpec(blk, lambda h, i: (i, 0, heads + h)),
            pl.BlockSpec(blk, lambda h, i: (i, 0, 2 * heads + h)),
            pl.BlockSpec((1, HEAD_DIM), lambda h, i: (0, 0)),
            pl.BlockSpec((1, HEAD_DIM), lambda h, i: (0, 0)),
            pl.BlockSpec((1, NA_WIN, GRID_W, NA_WIN * GRID_W), lambda h, i: (h, 0, 0, 0)),
        ],
        out_specs=pl.BlockSpec(blk, lambda h, i: (i, 0, h)),
        out_shape=jax.ShapeDtypeStruct((b, l, heads * HEAD_DIM), BF16),
        scratch_shapes=[pltpu.VMEM((l, HEAD_DIM), BF16), pltpu.VMEM((l, HEAD_DIM), BF16)],
        compiler_params=_cp(("parallel", "parallel")),
        name="na_attention",
    )(proj3, proj3, proj3, q_gain.reshape(1, HEAD_DIM).astype(F32),
      k_gain.reshape(1, HEAD_DIM).astype(F32), table)


def _filt_kernel(w1t_ref, w1c_ref, w1s_ref, b1_ref, w2_ref, b2_ref, w3_ref, b3_ref, freq_ref, ld_ref,
                 f_ref, sum_ref, *, l, tl):
    i = pl.program_id(0)
    t = (i * tl + lax.broadcasted_iota(jnp.int32, (tl, 1), 0)).astype(F32) / l
    bands = (lax.broadcasted_iota(jnp.int32, (1, HY_BANDS), 1) + 1).astype(F32)
    ang = 2.0 * math.pi * t * bands
    dot = functools.partial(jnp.dot, precision=HIGHEST, preferred_element_type=F32)
    pre = t * w1t_ref[...] + dot(jnp.cos(ang), w1c_ref[...]) + dot(jnp.sin(ang), w1s_ref[...]) + b1_ref[...]
    freq = freq_ref[...]
    hid = jnp.sin(freq * pre)
    hid = jnp.sin(freq * (dot(hid, w2_ref[...]) + b2_ref[...]))
    f = dot(hid, w3_ref[...]) + b3_ref[...]
    f = f * jnp.exp(-jnp.exp(ld_ref[...]) * t)
    f_ref[...] = f
    part = jnp.sum(jnp.abs(f), axis=0, keepdims=True)

    @pl.when(i == 0)
    def _():
        sum_ref[...] = part

    @pl.when(i > 0)
    def _():
        sum_ref[...] += part


def _filt_combine_kernel(f_ref, sum_ref, f1_ref, f2_ref, ny_ref, *, c, tl):
    i = pl.program_id(0)
    tot = sum_ref[:, :c] + sum_ref[:, c:]
    inv = 1.0 / tot
    n = i * tl + lax.broadcasted_iota(jnp.int32, (tl, 1), 0)
    hf = f_ref[:, :c] * inv
    hb = jnp.where(n == 0, 0.0, f_ref[:, c:] * inv)
    f1 = hf + hb
    f1_ref[...] = f1.astype(f1_ref.dtype)
    f2_ref[...] = (hb - hf).astype(f2_ref.dtype)
    sign = (1 - 2 * (n & 1)).astype(F32)
    part = jnp.sum(f1 * sign, axis=0, keepdims=True)

    @pl.when(i == 0)
    def _():
        ny_ref[...] = part

    @pl.when(i > 0)
    def _():
        ny_ref[...] += part


def hyena_filters(l, w1, b1, w2, b2, w3, b3, freq, log_decay):
    hid = w2.shape[0]
    c2 = w3.shape[1]
    c = c2 // 2
    tl = _pick(l, (256,))
    row = lambda a: a.reshape(1, -1).astype(F32)
    full = lambda shape: pl.BlockSpec(shape, lambda i: (0, 0))
    f, s = pl.pallas_call(
        functools.partial(_filt_kernel, l=l, tl=tl),
        grid=(l // tl,),
        in_specs=[full((1, hid)), full((HY_BANDS, hid)), full((HY_BANDS, hid)), full((1, hid)),
                  full((hid, hid)), full((1, hid)), full((hid, c2)), full((1, c2)), full((1, hid)), full((1, c2))],
        out_specs=[pl.BlockSpec((tl, c2), lambda i: (i, 0)), full((1, c2))],
        out_shape=[jax.ShapeDtypeStruct((l, c2), F32), jax.ShapeDtypeStruct((1, c2), F32)],
        compiler_params=_cp(("arbitrary",)),
        name="hyena_filter_mlp",
    )(w1[0:1].astype(F32), w1[1:1 + HY_BANDS].astype(F32), w1[1 + HY_BANDS:].astype(F32), row(b1),
      w2.astype(F32), row(b2), w3.astype(F32), row(b3), row(freq), row(log_decay))
    return pl.pallas_call(
        functools.partial(_filt_combine_kernel, c=c, tl=tl),
        grid=(l // tl,),
        in_specs=[pl.BlockSpec((tl, c2), lambda i: (i, 0)), full((1, c2))],
        out_specs=[pl.BlockSpec((tl, c), lambda i: (i, 0)), pl.BlockSpec((tl, c), lambda i: (i, 0)), full((1, c))],
        out_shape=[jax.ShapeDtypeStruct((l, c), BF16), jax.ShapeDtypeStruct((l, c), BF16),
                   jax.ShapeDtypeStruct((1, c), F32)],
        compiler_params=_cp(("arbitrary",)),
        name="hyena_filter_combine",
    )(f, s)


def _conv_gate_kernel(z0_ref, z1_ref, z2_ref, w0_ref, w1_ref, w2_ref, b0_ref, b1_ref, b2_ref,
                      u_ref, x0_ref, ny_ref, *, l, ch):
    tc = z0_ref.shape[-1]
    ny = jnp.zeros((1, tc), F32)

    def conv(z_ref, w_ref, b_ref, s):
        z = z_ref[0, s:s + ch, :].astype(F32)
        rid = lax.broadcasted_iota(jnp.int32, (ch, tc), 0)
        prev = z_ref[0, s - 1:s, :].astype(F32) if s > 0 else jnp.zeros((1, tc), F32)
        nxt = z_ref[0, s + ch:s + ch + 1, :].astype(F32) if s + ch < l else jnp.zeros((1, tc), F32)
        up = jnp.where(rid == 0, prev, pltpu.roll(z, 1, 0))
        dn = jnp.where(rid == ch - 1, nxt, pltpu.roll(z, ch - 1, 0))
        return w_ref[0:1, :] * up + w_ref[1:2, :] * z + w_ref[2:3, :] * dn + b_ref[...]

    for s in range(0, l, ch):
        x0 = conv(z0_ref, w0_ref, b0_ref, s)
        x1 = conv(z1_ref, w1_ref, b1_ref, s)
        v = conv(z2_ref, w2_ref, b2_ref, s)
        u = (v * x1).astype(u_ref.dtype)
        u_ref[0, s:s + ch, :] = u
        x0_ref[0, s:s + ch, :] = x0.astype(x0_ref.dtype)
        n = s + lax.broadcasted_iota(jnp.int32, (ch, 1), 0)
        ny = ny + jnp.sum(u.astype(F32) * (1 - 2 * (n & 1)).astype(F32), axis=0, keepdims=True)
    ny_ref[0] = ny


def hyena_conv_gate(proj3, conv_w, conv_b, base, c):
    b, l, _ = proj3.shape
    tc = _pick(math.gcd(base, c), (256, 128))
    ch = _pick(l, (512,))
    nb0, nbc = base // tc, c // tc
    zspec = lambda part: pl.BlockSpec((1, l, tc), lambda i, j: (i, 0, nb0 + part * nbc + j))
    wspec = lambda part: pl.BlockSpec((3, tc), lambda i, j: (0, part * nbc + j))
    bspec = lambda part: pl.BlockSpec((1, tc), lambda i, j: (0, part * nbc + j))
    ospec = pl.BlockSpec((1, l, tc), lambda i, j: (i, 0, j))
    cw = conv_w.astype(F32)
    cb = conv_b.reshape(1, -1).astype(F32)
    return pl.pallas_call(
        functools.partial(_conv_gate_kernel, l=l, ch=ch),
        grid=(b, nbc),
        in_specs=[zspec(0), zspec(1), zspec(2), wspec(0), wspec(1), wspec(2), bspec(0), bspec(1), bspec(2)],
        out_specs=[ospec, ospec, pl.BlockSpec((1, 1, tc), lambda i, j: (i, 0, j))],
        out_shape=[jax.ShapeDtypeStruct((b, l, c), BF16), jax.ShapeDtypeStruct((b, l, c), BF16),
                   jax.ShapeDtypeStruct((b, 1, c), F32)],
        compiler_params=_cp(("parallel", "parallel")),
        name="hyena_conv_gate",
    )(proj3, proj3, proj3, cw, cw, cw, cb, cb, cb)


def _dft_table_kernel(cb_ref, sb_ref, ck_ref, sk_ref, c_ref, s_ref):
    cb, sb = cb_ref[...], sb_ref[...]
    ck, sk = ck_ref[0], sk_ref[0]
    c_ref[...] = (ck * cb - sk * sb).astype(c_ref.dtype)
    s_ref[...] = (sk * cb + ck * sb).astype(s_ref.dtype)


def dft_tables(l):
    tk = _pick(l, (256,))

    def angles(k, t):
        return ((k * t) % (2 * l)).astype(F32) * (math.pi / l)

    t = lax.broadcasted_iota(jnp.int32, (tk, l), 1)
    base = angles(lax.broadcasted_iota(jnp.int32, (tk, l), 0), t)
    t0 = lax.broadcasted_iota(jnp.int32, (l // tk, 1, l), 2)
    first = angles(lax.broadcasted_iota(jnp.int32, (l // tk, 1, l), 0) * tk, t0)
    full = pl.BlockSpec((tk, l), lambda i: (0, 0))
    row = pl.BlockSpec((1, 1, l), lambda i: (i, 0, 0))
    out = pl.BlockSpec((tk, l), lambda i: (i, 0))
    return pl.pallas_call(
        _dft_table_kernel,
        grid=(l // tk,),
        in_specs=[full, full, row, row],
        out_specs=[out, out],
        out_shape=[jax.ShapeDtypeStruct((l, l), BF16)] * 2,
        compiler_params=_cp(("parallel",)),
        name="dft_tables",
    )(jnp.cos(base), jnp.sin(base), jnp.cos(first), jnp.sin(first))


def _dft_fwd_kernel(c_ref, s_ref, u_ref, tre_ref, tim_ref, z1_ref, z2_ref, *, l, tk):
    kb = pl.program_id(0)
    u = u_ref[0]
    p = jnp.dot(c_ref[...], u, preferred_element_type=F32)
    q = jnp.dot(s_ref[...], u, preferred_element_type=F32)
    kidx = kb * tk + lax.broadcasted_iota(jnp.int32, (tk, 1), 0)
    sc = jnp.where(kidx == 0, 0.5 / l, 1.0 / l)
    tre = tre_ref[...] * sc
    tim = tim_ref[...] * sc
    z1_ref[0] = (p * tre + q * tim).astype(z1_ref.dtype)
    z2_ref[0] = (q * tre - p * tim).astype(z2_ref.dtype)


def _dft_inv_kernel(c_ref, s_ref, z1_ref, z2_ref, ny_ref, tny_ref, u_ref, x0_ref, hb_ref, o_ref, *, l, tt):
    tb = pl.program_id(2)
    y = jnp.dot(c_ref[...], z1_ref[0], preferred_element_type=F32)
    y = y + jnp.dot(s_ref[...], z2_ref[0], preferred_element_type=F32)
    tidx = tb * tt + lax.broadcasted_iota(jnp.int32, (tt, 1), 0)
    sign = (1 - 2 * (tidx & 1)).astype(F32)
    y = y + sign * (ny_ref[0] * tny_ref[...] * (0.5 / l))
    y = y + u_ref[0].astype(F32) * hb_ref[...]
    o_ref[0] = (y * x0_ref[0].astype(F32)).astype(o_ref.dtype)


def hyena_long_conv(u, ny, x0, cmat, smat, tre, tim, tny, hy_bias):
    b, l, c = u.shape
    tc = _pick(c, (512, 256, 128))
    tk = _pick(l, (512,))
    z1, z2 = pl.pallas_call(
        functools.partial(_dft_fwd_kernel, l=l, tk=tk),
        grid=(l // tk, b, c // tc),
        in_specs=[
            pl.BlockSpec((tk, l), lambda kb, i, j: (kb, 0)),
            pl.BlockSpec((tk, l), lambda kb, i, j: (kb, 0)),
            pl.BlockSpec((1, l, tc), lambda kb, i, j: (i, 0, j)),
            pl.BlockSpec((tk, tc), lambda kb, i, j: (kb, j)),
            pl.BlockSpec((tk, tc), lambda kb, i, j: (kb, j)),
        ],
        out_specs=[
            pl.BlockSpec((1, tk, tc), lambda kb, i, j: (i, kb, j)),
            pl.BlockSpec((1, tk, tc), lambda kb, i, j: (i, kb, j)),
        ],
        out_shape=[jax.ShapeDtypeStruct((b, l, c), BF16), jax.ShapeDtypeStruct((b, l, c), BF16)],
        compiler_params=_cp(("parallel", "parallel", "parallel")),
        name="hyena_dft_fwd",
    )(cmat, smat, u, tre, tim)
    tt = tk
    return pl.pallas_call(
        functools.partial(_dft_inv_kernel, l=l, tt=tt),
        grid=(b, c // tc, l // tt),
        in_specs=[
            pl.BlockSpec((tt, l), lambda i, j, tb: (tb, 0)),
            pl.BlockSpec((tt, l), lambda i, j, tb: (tb, 0)),
            pl.BlockSpec((1, l, tc), lambda i, j, tb: (i, 0, j)),
            pl.BlockSpec((1, l, tc), lambda i, j, tb: (i, 0, j)),
            pl.BlockSpec((1, 1, tc), lambda i, j, tb: (i, 0, j)),
            pl.BlockSpec((1, tc), lambda i, j, tb: (0, j)),
            pl.BlockSpec((1, tt, tc), lambda i, j, tb: (i, tb, j)),
            pl.BlockSpec((1, tt, tc), lambda i, j, tb: (i, tb, j)),
            pl.BlockSpec((1, tc), lambda i, j, tb: (0, j)),
        ],
        out_specs=pl.BlockSpec((1, tt, tc), lambda i, j, tb: (i, tb, j)),
        out_shape=jax.ShapeDtypeStruct((b, l, c), BF16),
        compiler_params=_cp(("parallel", "parallel", "arbitrary")),
        name="hyena_dft_inv",
    )(cmat, smat, z1, z2, ny, tny, u, x0, hy_bias.reshape(1, c).astype(F32))


def _cmul(ar, ai, br, bi):
    return ar * br - ai * bi, ar * bi + ai * br


def _s5_prep_kernel(lre_ref, lim_ref, ls_ref, bre_ref, bim_ref, cre_ref, cim_ref,
                    pwr_ref, pwi_ref, wr_ref, wi_ref, clr_ref, cli_ref, k_ref):
    cg = bre_ref.shape[1]
    lr = jnp.minimum(lre_ref[...], -1e-4)
    li = lim_ref[...]
    step = jnp.exp(ls_ref[...])
    mag = jnp.exp(lr * step)
    ar = mag * jnp.cos(li * step)
    ai = mag * jnp.sin(li * step)
    den = lr * lr + li * li
    nr = ar - 1.0
    qr = (nr * lr + ai * li) / den
    qi = (ai * lr - nr * li) / den
    bbr, bbi = _cmul(qr, qi, bre_ref[...], bim_ref[...])
    cr = cre_ref[...]
    ci = cim_ref[...]
    pr = jnp.ones_like(ar)
    pi = jnp.zeros_like(ar)
    for tau in range(S5_T):
        sl = slice(tau * cg, (tau + 1) * cg)
        wr, wi = _cmul(pr, pi, bbr, bbi)
        wr_ref[:, sl, :] = wr
        wi_ref[:, sl, :] = wi
        pr, pi = _cmul(pr, pi, ar, ai)
        clr, cli = _cmul(cr, ci, pr, pi)
        clr_ref[:, sl, :] = clr
        cli_ref[:, sl, :] = cli
    qr, qi = pr, pi
    for m in range(8):
        pwr_ref[:, m:m + 1, :] = qr
        pwi_ref[:, m:m + 1, :] = qi
        qr, qi = _cmul(qr, qi, pr, pi)
    dims = (((2,), (2,)), ((0,), (0,)))
    k_ref[...] = (lax.dot_general(cr, wr_ref[...], dims, precision=HIGHEST, preferred_element_type=F32)
                  - lax.dot_general(ci, wi_ref[...], dims, precision=HIGHEST, preferred_element_type=F32))


def s5_prep(lam_re, lam_im, log_step, b_re, b_im, c_re, c_im):
    _, g, p = lam_re.shape
    cg = b_re.shape[-1]
    n = 2 * g
    tg = _pick(n, (8,))
    flat = lambda a: a.reshape(n, 1, p).astype(F32)
    bt = lambda a: a.reshape(n, p, cg).transpose(0, 2, 1).astype(F32)
    ct = lambda a: a.reshape(n, cg, p).astype(F32)
    s1 = pl.BlockSpec((tg, 1, p), lambda i: (i, 0, 0))
    s3 = pl.BlockSpec((tg, cg, p), lambda i: (i, 0, 0))
    s8 = pl.BlockSpec((tg, 8, p), lambda i: (i, 0, 0))
    st = pl.BlockSpec((tg, S5_T * cg, p), lambda i: (i, 0, 0))
    sk = pl.BlockSpec((tg, cg, S5_T * cg), lambda i: (i, 0, 0))
    big = jax.ShapeDtypeStruct((n, S5_T * cg, p), F32)
    return pl.pallas_call(
        _s5_prep_kernel,
        grid=(n // tg,),
        in_specs=[s1, s1, pl.BlockSpec((tg, 1, 1), lambda i: (i, 0, 0)), s3, s3, s3, s3],
        out_specs=[s8, s8, st, st, st, st, sk],
        out_shape=[jax.ShapeDtypeStruct((n, 8, p), F32)] * 2 + [big] * 4
                  + [jax.ShapeDtypeStruct((n, cg, S5_T * cg), F32)],
        compiler_params=_cp(("parallel",)),
        name="s5_discretise",
    )(flat(lam_re), flat(lam_im), log_step.reshape(n, 1, 1).astype(F32), bt(b_re), bt(b_im), ct(c_re), ct(c_im))


def _gelu(x):
    return 0.5 * x * (1.0 + jnp.tanh(math.sqrt(2.0 / math.pi) * (x + 0.044715 * (x * x * x))))


def _s5_kernel(u_ref, k_ref, b_ref, c_ref, tab_ref, o_ref, x_ref, xin_ref, *, s_sub, gt):
    nblk = u_ref.shape[1] // 8
    ns = x_ref.shape[-1] // 2
    half = ns // 2
    nsteps = (8 // s_sub).bit_length() - 1
    for g in range(gt):
        x_ref[g] = jnp.dot(u_ref[g], b_ref[g], preferred_element_type=F32)
    row = lax.broadcasted_iota(jnp.int32, (8, ns), 0)

    def scan_block(g, vr, vi, cr, ci, reverse):
        for k in range(nsteps):
            sh = s_sub << k
            shift = 8 - sh if reverse else sh
            ar, ai = tab_ref[g, 1 + k, :, :ns], tab_ref[g, 1 + k, :, ns:]
            dr, di = _cmul(ar, ai, pltpu.roll(vr, shift, 0), pltpu.roll(vi, shift, 0))
            vr, vi = vr + dr, vi + di
        if s_sub == 8:
            tr, ti = cr, ci
        else:
            src = (row < s_sub) if reverse else (row >= 8 - s_sub)
            tr = jnp.where(src, cr, 0.0)
            ti = jnp.where(src, ci, 0.0)
            for k in range(nsteps):
                tr = tr + pltpu.roll(tr, s_sub << k, 0)
                ti = ti + pltpu.roll(ti, s_sub << k, 0)
        dr, di = _cmul(tab_ref[g, 0, :, :ns], tab_ref[g, 0, :, ns:], tr, ti)
        fr, fi = vr + dr, vi + di
        if s_sub == 8:
            return tr, ti, fr, fi
        edge = (row >= 8 - s_sub) if reverse else (row < s_sub)
        shift = 8 - s_sub if reverse else s_sub
        xr = jnp.where(edge, tr, pltpu.roll(fr, shift, 0))
        xi = jnp.where(edge, ti, pltpu.roll(fi, shift, 0))
        return xr, xi, fr, fi

    def body(i, carry):
        of = pl.multiple_of(i * 8, 8)
        ob = pl.multiple_of((nblk - 1 - i) * 8, 8)
        new = []
        for g in range(gt):
            cfr, cfi, cbr, cbi = carry[4 * g:4 * g + 4]
            xr, xi, cfr, cfi = scan_block(g, x_ref[g, pl.ds(of, 8), :ns], x_ref[g, pl.ds(of, 8), ns:],
                                          cfr, cfi, False)
            xin_ref[g, pl.ds(of, 8), :half] = xr[:, :half]
            xin_ref[g, pl.ds(of, 8), ns:ns + half] = xi[:, :half]
            xr, xi, cbr, cbi = scan_block(g, x_ref[g, pl.ds(ob, 8), :ns], x_ref[g, pl.ds(ob, 8), ns:],
                                          cbr, cbi, True)
            xin_ref[g, pl.ds(ob, 8), half:ns] = xr[:, half:]
            xin_ref[g, pl.ds(ob, 8), ns + half:] = xi[:, half:]
            new += [cfr, cfi, cbr, cbi]
        return tuple(new)

    zero = jnp.zeros((8, ns), F32)
    lax.fori_loop(0, nblk, body, (zero,) * (4 * gt), unroll=2)
    for g in range(gt):
        y = jnp.dot(u_ref[g], k_ref[g], preferred_element_type=F32)
        y = y + jnp.dot(xin_ref[g].astype(BF16), c_ref[g], preferred_element_type=F32)
        o_ref[g] = _gelu(y).astype(o_ref.dtype)


def s5_weights(lam_re, lam_im, log_step, b_re, b_im, c_re, c_im, d_skip):
    _, g, p = lam_re.shape
    cg = b_re.shape[-1]
    t = S5_T
    pwr, pwi, wr, wi, clr, cli, kk = s5_prep(lam_re, lam_im, log_step, b_re, b_im, c_re, c_im)
    w4 = lambda a: a.reshape(2, g, t, cg, p)
    wr, wi, clr, cli = w4(wr), w4(wi), w4(clr), w4(cli)
    bm = [wr[0, :, ::-1], wr[1], wi[0, :, ::-1], wi[1]]
    bmat = jnp.concatenate([a.reshape(g, t * cg, p) for a in bm], axis=-1).astype(BF16)
    to_rows = lambda a: a.transpose(0, 3, 1, 2).reshape(g, p, t * cg)
    cm = [to_rows(clr[0]), to_rows(clr[1, :, ::-1]), -to_rows(cli[0]), -to_rows(cli[1, :, ::-1])]
    cmat = jnp.concatenate(cm, axis=1).astype(BF16)
    kk = kk.reshape(2, g, cg, t, cg)
    ts = jnp.arange(t)
    lag = ts[None, :] - ts[:, None]
    kf = jnp.where((lag >= 0)[None, None, :, :, None], kk[0][:, :, jnp.clip(lag, 0, t - 1), :], 0.0)
    kb = jnp.where((lag <= 0)[None, None, :, :, None], kk[1][:, :, jnp.clip(-lag, 0, t - 1), :], 0.0)
    skip = (d_skip.astype(F32).reshape(g, cg)[:, :, None, None, None]
            * jnp.eye(t, dtype=F32)[None, None, :, :, None] * jnp.eye(cg, dtype=F32)[None, :, None, None, :])
    kmat = (kf + kb + skip).transpose(0, 2, 4, 3, 1).reshape(g, t * cg, t * cg).astype(BF16)
    return kmat, bmat, cmat, (pwr.reshape(2, g, 8, p), pwi.reshape(2, g, 8, p))


def s5_scan_tables(pw, s_sub):
    pwr, pwi = pw
    nsteps = (8 // s_sub).bit_length() - 1
    rows = jnp.arange(8)
    tabs = []

    def entry(fwd_idx, bwd_idx, fwd_mask, bwd_mask):
        cols = [pwr[0][:, fwd_idx] * fwd_mask[None, :, None], pwr[1][:, bwd_idx] * bwd_mask[None, :, None],
                pwi[0][:, fwd_idx] * fwd_mask[None, :, None], pwi[1][:, bwd_idx] * bwd_mask[None, :, None]]
        return jnp.concatenate(cols, axis=-1)

    ones = jnp.ones((8,), F32)
    tabs.append(entry(rows // s_sub, (7 - rows) // s_sub, ones, ones))
    for k in range(nsteps):
        sh = s_sub << k
        idx = jnp.full((8,), (1 << k) - 1)
        tabs.append(entry(idx, idx, (rows >= sh).astype(F32), (rows < 8 - sh).astype(F32)))
    return jnp.stack(tabs, axis=1)


def _pack2(a, b):
    ua = lax.bitcast_convert_type(a.astype(BF16).astype(F32), jnp.int32)
    ub = lax.bitcast_convert_type(b.astype(BF16).astype(F32), jnp.int32)
    return ua | lax.shift_right_logical(ub, 16)


def _unpack2(p):
    a = lax.bitcast_convert_type(p & jnp.int32(-65536), F32)
    b = lax.bitcast_convert_type(lax.shift_left(p, 16), F32)
    return a, b


def _block_transpose(x):
    n = x.shape[1]
    rows = lax.broadcasted_iota(jnp.int32, x.shape, 0)
    lanes = lax.broadcasted_iota(jnp.int32, x.shape, 1)
    for k in range(4):
        s = 1 << k
        rbit = (rows >> k) & 1
        lbit = (lanes >> (4 + k)) & 1
        up = pltpu.roll(pltpu.roll(x, s, 0), n - 16 * s, 1)
        dn = pltpu.roll(pltpu.roll(x, x.shape[0] - s, 0), 16 * s, 1)
        x = jnp.where((rbit == 1) & (lbit == 0), up, jnp.where((rbit == 0) & (lbit == 1), dn, x))
    return x


def _s5_pack_kernel(x_ref, g_ref, o_ref, lo_ref, hi_ref, *, nb):
    d = x_ref.shape[1]
    hw = lo_ref.shape[1]
    w = 2 * hw
    nj = lo_ref.shape[0] // 16
    if nb is not None:
        live = pl.program_id(0) < nb
    x = x_ref[...]
    inv = lax.rsqrt(jnp.mean(x * x, axis=-1, keepdims=True) + EPS)
    for pair in range(d // (2 * w)):
        xs = []
        for blk in (2 * pair, 2 * pair + 1):
            sl = slice(blk * w, (blk + 1) * w)
            xn = x_ref[:, sl] * inv * g_ref[:, sl]
            if nb is not None:
                xn = jnp.where(live, xn, 0.0)
            xs.append(xn)
        t = _block_transpose(_pack2(*xs))
        lo_ref[...] = t[:, :hw]
        hi_ref[...] = t[:, hw:]
        for gl in range(16):
            lo = _unpack2(lo_ref[pl.ds(gl, nj, stride=16), :])
            hi = _unpack2(hi_ref[pl.ds(gl, nj, stride=16), :])
            for q in range(2):
                o_ref[(2 * pair + q) * 16 + gl, :, :hw] = lo[q].astype(o_ref.dtype)
                o_ref[(2 * pair + q) * 16 + gl, :, hw:] = hi[q].astype(o_ref.dtype)


def _s5_unpack_kernel(y_ref, o_ref, lo_ref, hi_ref):
    hw = lo_ref.shape[1]
    w = 2 * hw
    nj = lo_ref.shape[0] // 16
    for pair in range(o_ref.shape[1] // (2 * w)):
        for gl in range(16):
            yg = _pack2(y_ref[2 * pair * 16 + gl].astype(F32), y_ref[(2 * pair + 1) * 16 + gl].astype(F32))
            lo_ref[gl * nj:(gl + 1) * nj, :] = yg[:, :hw]
            hi_ref[gl * nj:(gl + 1) * nj, :] = yg[:, hw:]
        v = jnp.concatenate(
            [jnp.concatenate([lo_ref[pl.ds(j, 16, stride=nj), :], hi_ref[pl.ds(j, 16, stride=nj), :]], axis=1)
             for j in range(nj)], axis=0)
        t = _unpack2(_block_transpose(v))
        for q in range(2):
            o_ref[:, (2 * pair + q) * w:(2 * pair + q + 1) * w] = t[q].astype(o_ref.dtype)


def s5_mixer_core(x, gain, b, l, s5w, gt=4):
    d = x.shape[1]
    kmat, bmat, cmat, pw = s5w
    g = kmat.shape[0]
    cg = d // g
    t = S5_T
    nc = l // t
    nj = 16
    w = t * cg
    s_sub = 1 << (b - 1).bit_length()
    assert s_sub <= 8 and (nc * s_sub) % 8 == 0 and g % gt == 0 and cg == 16 and t == 16 and nc % nj == 0
    tab = s5_scan_tables(pw, s_sub)
    tok = nj * t
    lb = l // tok
    u = pl.pallas_call(
        functools.partial(_s5_pack_kernel, nb=None if s_sub == b else b),
        grid=(s_sub, lb),
        in_specs=[pl.BlockSpec((tok, d), lambda i, j: (jnp.minimum(i, b - 1) * lb + j, 0)),
                  pl.BlockSpec((1, d), lambda i, j: (0, 0))],
        out_specs=pl.BlockSpec((g, nj, w), lambda i, j: (0, j, i)),
        out_shape=jax.ShapeDtypeStruct((g, nc, s_sub * w), BF16),
        scratch_shapes=[pltpu.VMEM((nj * 16, w // 2), jnp.int32)] * 2,
        compiler_params=_cp(("parallel", "parallel")),
        name="s5_pack",
    )(x, gain.reshape(1, d).astype(F32))
    u = u.reshape(g, nc * s_sub, w)
    r = nc * s_sub
    ns4 = bmat.shape[-1]
    gspec = lambda shape: pl.BlockSpec((gt,) + shape, lambda i: (i,) + (0,) * len(shape))
    y = pl.pallas_call(
        functools.partial(_s5_kernel, s_sub=s_sub, gt=gt),
        grid=(g // gt,),
        in_specs=[gspec((r, w)), gspec((w, w)), gspec((w, ns4)), gspec((ns4, w)), gspec((tab.shape[1], 8, ns4))],
        out_specs=gspec((r, w)),
        out_shape=jax.ShapeDtypeStruct((g, r, w), BF16),
        scratch_shapes=[pltpu.VMEM((gt, r, ns4), F32)] * 2,
        compiler_params=_cp(("parallel",)),
        name="s5_chunked",
    )(u, kmat, bmat, cmat, tab)
    return pl.pallas_call(
        _s5_unpack_kernel,
        grid=(b, lb),
        in_specs=[pl.BlockSpec((g, nj, w), lambda i, j: (0, j, i))],
        out_specs=pl.BlockSpec((tok, d), lambda i, j: (i * lb + j, 0)),
        out_shape=jax.ShapeDtypeStruct((b * l, d), BF16),
        scratch_shapes=[pltpu.VMEM((nj * 16, w // 2), jnp.int32)] * 2,
        compiler_params=_cp(("parallel", "parallel")),
        name="s5_unpack",
    )(y.reshape(g, nc, s_sub * w))


def _xattn_kernel(q_ref, k_ref, v_ref, qg_ref, kg_ref, o_ref, *, heads):
    scale = HEAD_DIM ** -0.5

    def hnorm(x, g_ref):
        ms = jnp.mean(x * x, axis=-1, keepdims=True)
        return (x * lax.rsqrt(ms + EPS) * g_ref[...]).astype(BF16)

    for h in range(heads):
        sl = slice(h * HEAD_DIM, (h + 1) * HEAD_DIM)
        q = hnorm(q_ref[0, :, sl], qg_ref)
        k = hnorm(k_ref[0, :, sl], kg_ref)
        s = lax.dot_general(q, k, (((1,), (1,)), ((), ())), preferred_element_type=F32) * scale
        m = jnp.max(s, axis=-1, keepdims=True)
        p = jnp.exp(s - m)
        den = jnp.sum(p, axis=-1, keepdims=True)
        o = jnp.dot(p.astype(BF16), v_ref[0, :, sl].astype(BF16), preferred_element_type=F32)
        o_ref[0, :, sl] = (o * (1.0 / den)).astype(o_ref.dtype)


def cross_attention_core(q, k, v, q_gain, k_gain):
    b, l, xw = q.shape
    mm = k.shape[1]
    tq = _pick(l, (512, 256, 128))
    g = lambda a: a.reshape(1, HEAD_DIM).astype(F32)
    return pl.pallas_call(
        functools.partial(_xattn_kernel, heads=xw // HEAD_DIM),
        grid=(b, l // tq),
        in_specs=[
            pl.BlockSpec((1, tq, xw), lambda i, j: (i, j, 0)),
            pl.BlockSpec((1, mm, xw), lambda i, j: (i, 0, 0)),
            pl.BlockSpec((1, mm, xw), lambda i, j: (i, 0, 0)),
            pl.BlockSpec((1, HEAD_DIM), lambda i, j: (0, 0)),
            pl.BlockSpec((1, HEAD_DIM), lambda i, j: (0, 0)),
        ],
        out_specs=pl.BlockSpec((1, tq, xw), lambda i, j: (i, j, 0)),
        out_shape=jax.ShapeDtypeStruct((b, l, xw), BF16),
        compiler_params=_cp(("parallel", "parallel")),
        name="cross_attention",
    )(q, k, v, g(q_gain), g(k_gain))


def kernel(x_prompt, x_sample, mem_prompt, mem_sample, g_mix, g_cross, g_mem, g_ffn, w_in, na_q_gain, na_k_gain, na_rpb, hy_conv_w, hy_conv_b, hy_f_w1, hy_f_b1, hy_f_w2, hy_f_b2, hy_f_w3, hy_f_b3, hy_f_freq, hy_log_decay, hy_bias, w_out, s5_lam_re, s5_lam_im, s5_log_step, s5_b_re, s5_b_im, s5_c_re, s5_c_im, s5_d, w_glu, x_wq, x_wk, x_wv, x_wo, x_q_gain, x_k_gain, w_ffn_gate, w_ffn_up, w_ffn_down):
    depth, d = g_mix.shape
    heads = na_rpb.shape[1]
    naw = heads * HEAD_DIM
    hyw = hy_bias.shape[-1]
    bf = lambda w: w.astype(BF16)

    layers = []
    for layer in range(depth):
        i = layer // 2
        p = {}
        if layer % 2 == 0:
            p["w_in"] = cast_weight(w_in, i)
            p["w_out"] = cast_weight(w_out, i)
            p["table"] = na_table(na_rpb[i])
            p["filters"] = {}
        else:
            p["s5w"] = s5_weights(s5_lam_re[i], s5_lam_im[i], s5_log_step[i], s5_b_re[i], s5_b_im[i],
                                  s5_c_re[i], s5_c_im[i], s5_d[i])
            p["w_glu"] = cast_weight(w_glu, i)
        p["wq"], p["wk"] = cast_weight(x_wq, layer), cast_weight(x_wk, layer)
        p["wv"], p["wo"] = cast_weight(x_wv, layer), cast_weight(x_wo, layer)
        p["ffn"] = (cast_weight(w_ffn_gate, layer), cast_weight(w_ffn_up, layer), cast_weight(w_ffn_down, layer))
        layers.append(p)
    tables = {}

    def filters_for(layer, l):
        p = layers[layer]
        if l not in p["filters"]:
            i = layer // 2
            if l not in tables:
                tables[l] = dft_tables(l)
            cmat, smat = tables[l]
            f1, f2, ny = hyena_filters(l, hy_f_w1[i], hy_f_b1[i], hy_f_w2[i], hy_f_b2[i], hy_f_w3[i],
                                       hy_f_b3[i], hy_f_freq[i], hy_log_decay[i].reshape(-1))
            tre = matmul([cmat], [(f1, 0)], f1.shape[1])
            tim = matmul([smat], [(f2, 0)], f2.shape[1])
            p["filters"][l] = (cmat, smat, tre, tim, ny)
        return p["filters"][l]

    def run(x, mem):
        b, l, _ = x.shape
        nm = mem.shape[1]
        x = x.reshape(b * l, d)
        mem2 = mem.reshape(b * nm, d)
        for layer in range(depth):
            i = layer // 2
            p = layers[layer]
            if layer % 2 == 0:
                xn = rmsnorm(x, g_mix[layer], BF16)
                proj = matmul([xn], [(p["w_in"], 0)], p["w_in"].shape[1], out_dtype=BF16)
                proj3 = proj.reshape(b, l, -1)
                y_a = na_attention(proj3, na_q_gain[i], na_k_gain[i], p["table"], heads)
                cmat, smat, tre, tim, tny = filters_for(layer, l)
                u, x0, ny = hyena_conv_gate(proj3, hy_conv_w[i], hy_conv_b[i], 3 * naw, hyw)
                y_b = hyena_long_conv(u, ny, x0, cmat, smat, tre, tim, tny, hy_bias[i])
                y_a, y_b = y_a.reshape(b * l, naw), y_b.reshape(b * l, hyw)
                ymix = [y_a, y_b] if naw == hyw else [jnp.concatenate([y_a, y_b], axis=-1)]
                x = matmul(ymix, [(p["w_out"], 0)], d, epi="res", res=x)
            else:
                y = s5_mixer_core(x, g_mix[layer], b, l, p["s5w"])
                x = matmul([y], [(p["w_glu"], 0), (p["w_glu"], d)], d, epi="glu_res", res=x)
            mn = rmsnorm(mem2, g_mem[layer], BF16)
            xw = p["wq"].shape[1]
            q = matmul([x], [(p["wq"], 0)], xw, norm_gain=g_cross[layer])
            k = matmul([mn], [(p["wk"], 0)], xw)
            v = matmul([mn], [(p["wv"], 0)], xw)
            o = cross_attention_core(q.reshape(b, l, xw), k.reshape(b, nm, xw), v.reshape(b, nm, xw),
                                     x_q_gain[layer], x_k_gain[layer])
            x = matmul([o.reshape(b * l, xw)], [(p["wo"], 0)], d, epi="res", res=x)
            wg, wu, wd = p["ffn"]
            xn = rmsnorm(x, g_ffn[layer], BF16)
            hdn = matmul([xn], [(wg, 0), (wu, 0)], wg.shape[1], epi="swiglu", out_dtype=BF16)
            x = matmul([hdn], [(wd, 0)], d, epi="res", res=x)
        return x.reshape(b, l, d)

    return run(x_prompt, mem_prompt), run(x_sample, mem_sample)
```
